```python
import math
import jax, jax.numpy as jnp
from jax import lax
import numpy as np

D_MODEL = 1024
BATCH = 4
SEQ = 8192
DEPTH = 1

D_MIX = 2 * D_MODEL
GLA_HEADS = 4
GLA_DV = D_MIX // 2
GLA_DK = GLA_DV // 2
GLA_HEAD_K = GLA_DK // GLA_HEADS
GLA_HEAD_V = GLA_DV // GLA_HEADS
GLA_GATE_RANK = 16
GLA_GATE_TAU = 16.0
GLA_CHUNK = 64
SSD_DINNER = D_MIX - GLA_DV
SSD_HEADDIM = 64
SSD_HEADS = SSD_DINNER // SSD_HEADDIM
SSD_GROUPS = 2
SSD_HPG = SSD_HEADS // SSD_GROUPS
SSD_STATE = 64
SSD_CONV = 4
SSD_CHUNK = 64
SSD_CONV_DIM = SSD_DINNER + 2 * SSD_GROUPS * SSD_STATE
IN_SIZES = (GLA_DK, GLA_DK, GLA_DV, GLA_DV, GLA_GATE_RANK, SSD_DINNER, SSD_CONV_DIM, SSD_HEADS)
D_IN_PROJ = GLA_DK * 2 + GLA_DV * 2 + GLA_GATE_RANK + SSD_DINNER + SSD_CONV_DIM + SSD_HEADS
MEM_LEN = 256
MEM_HEADS = 4
MEM_HEAD_DIM = D_MODEL // MEM_HEADS
PEER_HEADS = 8
PEER_N_KEYS = 128
PEER_EXPERTS = PEER_N_KEYS * PEER_N_KEYS
PEER_KEY_DIM = 128
PEER_HALF = PEER_KEY_DIM // 2
PEER_TOPK = 16
PEER_BLOCK = 64
DEEPNORM_ALPHA = (2.0 * DEPTH) ** 0.25
DEEPNORM_BETA = (8.0 * DEPTH) ** -0.25
LN_EPS = 1e-5
RMS_EPS = 1e-6

kernel_name = 'hybrid_gla_ssd_peer_deepnorm'


def _split_points(sizes):
    pts, acc = [], 0
    for s in sizes[:-1]:
        acc += s
        pts.append(acc)
    return pts


def layer_norm(x, g, b):
    xf = x.astype(jnp.float32)
    mu = xf.mean(-1, keepdims=True)
    var = jnp.square(xf - mu).mean(-1, keepdims=True)
    return ((xf - mu) * lax.rsqrt(var + LN_EPS) * g + b).astype(x.dtype)


def rms_normalize(x):
    xf = x.astype(jnp.float32)
    return xf * lax.rsqrt(jnp.mean(jnp.square(xf), -1, keepdims=True) + RMS_EPS)


def causal_depthwise_conv(x, w, b):
    c = x.shape[-1]
    y = lax.conv_general_dilated(x, w[:, None, :], window_strides=(1,),
                                 padding=[(SSD_CONV - 1, 0)],
                                 dimension_numbers=('NWC', 'WIO', 'NWC'),
                                 feature_group_count=c)
    return y + b


def gla_chunked(q, k, v, log_a):
    bsz, s, nh, dk = q.shape
    dv = v.shape[-1]
    L = GLA_CHUNK
    nc = s // L

    def to_chunks(t):
        return t.reshape(bsz, nc, L, nh, t.shape[-1]).transpose(1, 0, 3, 2, 4)

    qc, kc, vc, gc = to_chunks(q), to_chunks(k), to_chunks(v), to_chunks(log_a)
    causal = jnp.tril(jnp.ones((L, L), dtype=bool))

    def step(state, inp):
        qi, ki, vi, gi = inp
        b = jnp.cumsum(gi.astype(jnp.float32), axis=2)
        diff = b[:, :, :, None, :] - b[:, :, None, :, :]
        decay = jnp.exp(jnp.where(causal[:, :, None], diff, -jnp.inf))
        scores = jnp.einsum('bhtd,bhsd,bhtsd->bhts', qi, ki, decay)
        o = (jnp.einsum('bhts,bhse->bhte', scores, vi)
             + jnp.einsum('bhtd,bhde->bhte', qi * jnp.exp(b), state))
        b_last = b[:, :, -1:, :]
        state = (jnp.exp(b_last[:, :, 0, :, None]) * state
                 + jnp.einsum('bhsd,bhse->bhde', ki * jnp.exp(b_last - b), vi))
        return state, o

    state0 = jnp.zeros((bsz, nh, dk, dv), jnp.float32)
    _, o = lax.scan(step, state0, (qc, kc, vc, gc))
    return o.transpose(1, 0, 3, 2, 4).reshape(bsz, s, nh, dv)


def segsum(a):
    L = a.shape[-1]
    cs = jnp.cumsum(a, -1)
    diff = cs[..., :, None] - cs[..., None, :]
    mask = jnp.tril(jnp.ones((L, L), dtype=bool))
    return jnp.where(mask, diff, -jnp.inf)


def ssd_chunked(x, dt, a_neg, bm, cm):
    bsz, s, ng, nr, p = x.shape
    n = bm.shape[-1]
    L = SSD_CHUNK
    nc = s // L
    x = x.reshape(bsz, nc, L, ng, nr, p)
    dt = dt.reshape(bsz, nc, L, ng, nr)
    bm = bm.reshape(bsz, nc, L, ng, n)
    cm = cm.reshape(bsz, nc, L, ng, n)
    a = (dt * a_neg).transpose(0, 1, 3, 4, 2)
    a_cum = jnp.cumsum(a, -1)
    lmat = jnp.exp(segsum(a))
    xdt = x * dt[..., None]
    cb = jnp.einsum('bclgn,bcsgn->bcgls', cm, bm)
    y_diag = jnp.einsum('bcgls,bcgrls,bcsgrp->bclgrp', cb, lmat, xdt)
    decay_states = jnp.exp(a_cum[..., -1:] - a_cum)
    states = jnp.einsum('bcsgn,bcgrs,bcsgrp->bcgrpn', bm, decay_states, xdt)
    chunk_decay = jnp.exp(a_cum[..., -1])

    def step(h, inp):
        st, dec = inp
        return dec[..., None, None] * h + st, h

    h0 = jnp.zeros((bsz, ng, nr, p, n), jnp.float32)
    _, h_prev = lax.scan(step, h0, (states.transpose(1, 0, 2, 3, 4, 5), chunk_decay.transpose(1, 0, 2, 3)))
    h_prev = h_prev.transpose(1, 0, 2, 3, 4, 5)
    y_off = jnp.einsum('bclgn,bcgrpn,bcgrl->bclgrp', cm, h_prev, jnp.exp(a_cum))
    return (y_diag + y_off).reshape(bsz, s, ng, nr, p)


def hybrid_mixer(h, w_in, gla_wg2, gla_bg, gla_norm_g, conv_w, conv_b, dt_bias, a_log, d_skip,
                 ssd_norm_g, w_out):
    bsz, s, _ = h.shape
    proj = h @ w_in
    q, k, v, r, g_low, z, xbc, dt_raw = jnp.split(proj, _split_points(IN_SIZES), axis=-1)
    q = q.reshape(bsz, s, GLA_HEADS, GLA_HEAD_K) * (GLA_HEAD_K ** -0.5)
    k = k.reshape(bsz, s, GLA_HEADS, GLA_HEAD_K)
    v = v.reshape(bsz, s, GLA_HEADS, GLA_HEAD_V)
    log_a = jax.nn.log_sigmoid((g_low @ gla_wg2 + gla_bg).astype(jnp.float32)) / GLA_GATE_TAU
    log_a = log_a.reshape(bsz, s, GLA_HEADS, GLA_HEAD_K)
    o = gla_chunked(q, k, v, log_a)
    o = rms_normalize(o) * gla_norm_g * jax.nn.silu(r.reshape(bsz, s, GLA_HEADS, GLA_HEAD_V).astype(jnp.float32))
    o = o.reshape(bsz, s, GLA_DV).astype(h.dtype)
    xbc = jax.nn.silu(causal_depthwise_conv(xbc, conv_w, conv_b))
    xs, bm, cm = jnp.split(xbc, [SSD_DINNER, SSD_DINNER + SSD_GROUPS * SSD_STATE], axis=-1)
    xs = xs.reshape(bsz, s, SSD_GROUPS, SSD_HPG, SSD_HEADDIM)
    bm = bm.reshape(bsz, s, SSD_GROUPS, SSD_STATE)
    cm = cm.reshape(bsz, s, SSD_GROUPS, SSD_STATE)
    dt = jax.nn.softplus((dt_raw + dt_bias).astype(jnp.float32)).reshape(bsz, s, SSD_GROUPS, SSD_HPG)
    a_neg = -jnp.exp(a_log.astype(jnp.float32)).reshape(SSD_GROUPS, SSD_HPG)
    y = ssd_chunked(xs, dt, a_neg, bm, cm) + d_skip.reshape(SSD_GROUPS, SSD_HPG)[..., None] * xs
    y = y.reshape(bsz, s, SSD_DINNER) * jax.nn.silu(z.astype(jnp.float32))
    y = rms_normalize(y.reshape(bsz, s, SSD_GROUPS, SSD_DINNER // SSD_GROUPS)).reshape(bsz, s, SSD_DINNER)
    y = (y * ssd_norm_g).astype(h.dtype)
    return jnp.concatenate([o, y], axis=-1) @ w_out


def memory_cross_attention(h, mem, wq, wk, wv, wo):
    bsz, s, _ = h.shape
    m = mem.shape[1]
    q = (h @ wq).reshape(bsz, s, MEM_HEADS, MEM_HEAD_DIM)
    k = (mem @ wk).reshape(bsz, m, MEM_HEADS, MEM_HEAD_DIM)
    v = (mem @ wv).reshape(bsz, m, MEM_HEADS, MEM_HEAD_DIM)
    scores = jnp.einsum('bshd,bmhd->bhsm', q, k).astype(jnp.float32) * (MEM_HEAD_DIM ** -0.5)
    p = jax.nn.softmax(scores, axis=-1).astype(v.dtype)
    o = jnp.einsum('bhsm,bmhd->bshd', p, v).reshape(bsz, s, MEM_HEADS * MEM_HEAD_DIM)
    return o @ wo


def peer_ffn(h, w_q, sub_keys, u, v):
    bsz, s, d = h.shape
    q = (h @ w_q).reshape(bsz, s, PEER_HEADS, 2, PEER_HALF).astype(jnp.float32)
    scores = jnp.einsum('bshpd,pkd->bshpk', q, sub_keys.astype(jnp.float32))
    s_top, i_top = lax.top_k(scores, PEER_TOPK)
    cand = (s_top[..., 0, :, None] + s_top[..., 1, None, :]).reshape(bsz, s, PEER_HEADS, PEER_TOPK * PEER_TOPK)
    best, c_idx = lax.top_k(cand, PEER_TOPK)
    i1 = jnp.take_along_axis(i_top[..., 0, :], c_idx // PEER_TOPK, axis=-1)
    i2 = jnp.take_along_axis(i_top[..., 1, :], c_idx % PEER_TOPK, axis=-1)
    expert = i1 * PEER_N_KEYS + i2
    gates = jax.nn.softmax(best, axis=-1)
    nb = s // PEER_BLOCK

    def blocks(t):
        return t.reshape(bsz, nb, PEER_BLOCK, *t.shape[2:]).swapaxes(0, 1)

    def per_block(inp):
        xb, eb, gb = inp
        act = jnp.einsum('bthkd,btd->bthk', u[eb], xb).astype(jnp.float32)
        w = (gb * jax.nn.gelu(act, approximate=False)).astype(v.dtype)
        return jnp.einsum('bthk,bthkd->btd', w, v[eb])

    out = lax.map(per_block, (blocks(h), blocks(expert), blocks(gates)))
    return out.swapaxes(0, 1).reshape(bsz, s, d)


def setup_inputs(seed: int = 0) -> dict:
    key = jax.random.key(seed)
    ks = jax.random.split(key, 32)
    f32 = jnp.float32

    def nrm(k, shape, scale):
        return jax.random.normal(k, shape, f32) * scale

    def gain(k, n):
        return 1.0 + nrm(k, (DEPTH, n), 0.02)

    dt0 = jnp.exp(jax.random.uniform(ks[7], (DEPTH, SSD_HEADS), f32, math.log(1e-3), math.log(1e-1)))
    return {
        'x': nrm(ks[0], (BATCH, SEQ, D_MODEL), 1.0),
        'mem': nrm(ks[1], (BATCH, MEM_LEN, D_MODEL), 1.0),
        'w_in': nrm(ks[2], (DEPTH, D_MODEL, D_IN_PROJ), D_MODEL ** -0.5),
        'gla_wg2': nrm(ks[3], (DEPTH, GLA_GATE_RANK, GLA_DK), GLA_GATE_RANK ** -0.5),
        'gla_bg': nrm(ks[4], (DEPTH, GLA_DK), 0.1),
        'gla_norm_g': gain(ks[5], GLA_HEAD_V),
        'ssd_conv_w': nrm(ks[6], (DEPTH, SSD_CONV, SSD_CONV_DIM), SSD_CONV ** -0.5),
        'ssd_conv_b': nrm(ks[8], (DEPTH, SSD_CONV_DIM), 0.02),
        'ssd_dt_bias': dt0 + jnp.log(-jnp.expm1(-dt0)),
        'ssd_a_log': jnp.log(jax.random.uniform(ks[9], (DEPTH, SSD_HEADS), f32, 1.0, 16.0)),
        'ssd_d': 1.0 + nrm(ks[10], (DEPTH, SSD_HEADS), 0.02),
        'ssd_norm_g': gain(ks[11], SSD_DINNER),
        'w_out': nrm(ks[12], (DEPTH, D_MIX, D_MODEL), DEEPNORM_BETA * D_MIX ** -0.5),
        'ln1_g': gain(ks[13], D_MODEL),
        'ln1_b': nrm(ks[14], (DEPTH, D_MODEL), 0.02),
        'ca_wq': nrm(ks[15], (DEPTH, D_MODEL, D_MODEL), D_MODEL ** -0.5),
        'ca_wk': nrm(ks[16], (DEPTH, D_MODEL, D_MODEL), D_MODEL ** -0.5),
        'ca_wv': nrm(ks[17], (DEPTH, D_MODEL, D_MODEL), D_MODEL ** -0.5),
        'ca_wo': nrm(ks[18], (DEPTH, D_MODEL, D_MODEL), DEEPNORM_BETA * D_MODEL ** -0.5),
        'ln2_g': gain(ks[19], D_MODEL),
        'ln2_b': nrm(ks[20], (DEPTH, D_MODEL), 0.02),
        'peer_wq': nrm(ks[21], (DEPTH, D_MODEL, PEER_HEADS * PEER_KEY_DIM), D_MODEL ** -0.5),
        'peer_keys': nrm(ks[22], (DEPTH, 2, PEER_N_KEYS, PEER_HALF), PEER_HALF ** -0.5),
        'peer_u': nrm(ks[23], (DEPTH, PEER_EXPERTS, D_MODEL), D_MODEL ** -0.5),
        'peer_v': nrm(ks[24], (DEPTH, PEER_EXPERTS, D_MODEL), DEEPNORM_BETA),
        'ln3_g': gain(ks[25], D_MODEL),
        'ln3_b': nrm(ks[26], (DEPTH, D_MODEL), 0.02),
    }


def reference(x, mem, w_in, gla_wg2, gla_bg, gla_norm_g, ssd_conv_w, ssd_conv_b, ssd_dt_bias,
              ssd_a_log, ssd_d, ssd_norm_g, w_out, ln1_g, ln1_b, ca_wq, ca_wk, ca_wv, ca_wo,
              ln2_g, ln2_b, peer_wq, peer_keys, peer_u, peer_v, ln3_g, ln3_b):
    h = x
    for l in range(DEPTH):
        mix = hybrid_mixer(h, w_in[l], gla_wg2[l], gla_bg[l], gla_norm_g[l], ssd_conv_w[l],
                           ssd_conv_b[l], ssd_dt_bias[l], ssd_a_log[l], ssd_d[l], ssd_norm_g[l], w_out[l])
        h = layer_norm(DEEPNORM_ALPHA * h + mix, ln1_g[l], ln1_b[l])
        ca = memory_cross_attention(h, mem, ca_wq[l], ca_wk[l], ca_wv[l], ca_wo[l])
        h = layer_norm(DEEPNORM_ALPHA * h + ca, ln2_g[l], ln2_b[l])
        ff = peer_ffn(h, peer_wq[l], peer_keys[l], peer_u[l], peer_v[l])
        h = layer_norm(DEEPNORM_ALPHA * h + ff, ln3_g[l], ln3_b[l])
    return h
```

```python
import functools
import math

import jax
import jax.numpy as jnp
from jax import lax
from jax.experimental import pallas as pl
from jax.experimental.pallas import tpu as pltpu

F32 = jnp.float32
BF16 = jnp.bfloat16
HIGHEST = lax.Precision.HIGHEST

D_MODEL = 1024
GLA_HEADS = 4
GLA_HEAD_K = 128
GLA_HEAD_V = 256
GLA_DK = GLA_HEADS * GLA_HEAD_K
GLA_DV = GLA_HEADS * GLA_HEAD_V
GLA_GATE_RANK = 16
GLA_GATE_TAU = 16.0
CHUNK = 64
SSD_DINNER = 1024
SSD_HEADDIM = 64
SSD_HEADS = 16
SSD_GROUPS = 2
SSD_HPG = SSD_HEADS // SSD_GROUPS
SSD_STATE = 64
SSD_CONV = 4
SSD_CONV_DIM = SSD_DINNER + 2 * SSD_GROUPS * SSD_STATE
SSD_GROUP_W = SSD_DINNER // SSD_GROUPS
MEM_HEADS = 4
MEM_HEAD_DIM = 256
PEER_HEADS = 8
PEER_N_KEYS = 128
PEER_HALF = 64
PEER_TOPK = 16
DEPTH = 1
DEEPNORM_ALPHA = (2.0 * DEPTH) ** 0.25
LN_EPS = 1e-5
RMS_EPS = 1e-6

LANES = 128
SUBLANES = 8
VMEM_LIMIT = 56 * 1024 * 1024

SSD_BC_W = SSD_CONV_DIM - SSD_DINNER
COL_Z = 0
COL_XS = COL_Z + SSD_DINNER
COL_V = COL_XS + SSD_DINNER
COL_R = COL_V + GLA_DV
COL_Q = COL_R + GLA_DV
COL_K = COL_Q + GLA_DK
COL_BC = COL_K + GLA_DK
COL_GLOW = COL_BC + SSD_BC_W
COL_DT = COL_GLOW + LANES
PROJ_W = COL_DT + LANES


def _cparams(sem):
    return pltpu.CompilerParams(dimension_semantics=sem, vmem_limit_bytes=VMEM_LIMIT)


def _dot(a, b):
    return jnp.dot(a, b, preferred_element_type=F32)


def _dot_nt(a, b):
    return lax.dot_general(a, b, (((1,), (1,)), ((), ())), preferred_element_type=F32)


def _dot_tn(a, b):
    return lax.dot_general(a, b, (((0,), (0,)), ((), ())), preferred_element_type=F32)


def _dot_exact(a, b):
    return jnp.dot(a, b, preferred_element_type=F32, precision=HIGHEST)


def _silu(x):
    return x / (1.0 + jnp.exp(-x))


def _softplus(x):
    return jnp.maximum(x, 0.0) + jnp.log1p(jnp.exp(-jnp.abs(x)))


def _layer_norm(x, g, b):
    mu = jnp.mean(x, axis=-1, keepdims=True)
    xc = x - mu
    var = jnp.mean(xc * xc, axis=-1, keepdims=True)
    return xc * lax.rsqrt(var + LN_EPS) * g + b


def _matmul_kernel(x_ref, w_ref, o_ref):
    o_ref[...] = _dot(x_ref[...].astype(BF16), w_ref[...])


def _matmul(x, w, tm, tn):
    m, k = x.shape
    n = w.shape[1]
    return pl.pallas_call(
        _matmul_kernel,
        grid=(m // tm, n // tn),
        in_specs=[pl.BlockSpec((tm, k), lambda i, j: (i, 0)),
                  pl.BlockSpec((k, tn), lambda i, j: (0, j))],
        out_specs=pl.BlockSpec((tm, tn), lambda i, j: (i, j)),
        out_shape=jax.ShapeDtypeStruct((m, n), F32),
        compiler_params=_cparams(("arbitrary", "arbitrary")),
        name="matmul",
    )(x, w)


GLA_LEVELS = (32, 16, 8, 4, 2, 1)


def _gla_select_matrix():
    t = jnp.arange(CHUNK)
    tril = (t[:, None] >= t[None, :]).astype(F32)
    mats = [tril]
    for c in GLA_LEVELS:
        mid = (t // (2 * c)) * (2 * c) + c
        mats.append(tril[mid])
    return jnp.concatenate(mats, axis=0)


def _gla_kernel(q_ref, k_ref, v_ref, r_ref, gl_ref, wg_ref, bg_ref, gn_ref, sel_ref, o_ref,
                state_ref, loga_ref, *, tc):
    @pl.when(pl.program_id(2) == 0)
    def _():
        state_ref[...] = jnp.zeros_like(state_ref)

    gate_pre = _dot(gl_ref[...].astype(BF16), wg_ref[0]) + bg_ref[0]
    loga_ref[...] = -_softplus(-gate_pre) * (1.0 / GLA_GATE_TAU)

    row = lax.broadcasted_iota(jnp.int32, (CHUNK, CHUNK), 0)
    col = lax.broadcasted_iota(jnp.int32, (CHUNK, CHUNK), 1)
    ones_dv = jnp.ones((CHUNK, GLA_HEAD_V), F32)
    scale = GLA_HEAD_K ** -0.5

    def chunk(c, carry):
        r0 = pl.multiple_of(c * CHUNK, CHUNK)
        la = loga_ref[pl.ds(r0, CHUNK), :]
        ball = _dot_exact(sel_ref[...], la)
        b = ball[0:CHUNK]
        q = q_ref[pl.ds(r0, CHUNK), :] * scale
        k = k_ref[pl.ds(r0, CHUNK), :]
        v = v_ref[pl.ds(r0, CHUNK), :].astype(BF16)
        scores = jnp.where(row == col, _dot_nt(q.astype(BF16), k.astype(BF16)), 0.0)
        for li, cs in enumerate(GLA_LEVELS):
            bm = ball[(li + 1) * CHUNK:(li + 2) * CHUNK]
            qe = q * jnp.exp(jnp.minimum(b - bm, 0.0))
            ke = k * jnp.exp(jnp.minimum(bm - b, 0.0))
            p = _dot_nt(qe.astype(BF16), ke.astype(BF16))
            same_blk = (row // (2 * cs)) == (col // (2 * cs))
            mask = same_blk & ((row % (2 * cs)) >= cs) & ((col % (2 * cs)) < cs)
            scores = jnp.where(mask, p, scores)
        state = state_ref[...]
        o = _dot(scores.astype(BF16), v) + _dot((q * jnp.exp(b)).astype(BF16), state.astype(BF16))
        b_last = b[CHUNK - 1:CHUNK, :]
        kd = k * jnp.exp(b_last - b)
        decay = jnp.exp(lax.dot_general(la, ones_dv, (((0,), (0,)), ((), ())),
                                        preferred_element_type=F32, precision=HIGHEST))
        state_ref[...] = decay * state + _dot_tn(kd.astype(BF16), v)
        o = o * lax.rsqrt(jnp.mean(o * o, axis=-1, keepdims=True) + RMS_EPS)
        o_ref[pl.ds(r0, CHUNK), :] = o * gn_ref[...] * _silu(r_ref[pl.ds(r0, CHUNK), :])
        return carry

    lax.fori_loop(0, tc // CHUNK, chunk, 0)


def _gla(proj, wg_pad, bg, gn, bsz, seq, tc):
    nt = seq // tc
    t = bsz * seq
    kern = functools.partial(_gla_kernel, tc=tc)

    def tok(b, h, i):
        return b * nt + i

    return pl.pallas_call(
        kern,
        grid=(bsz, GLA_HEADS, nt),
        in_specs=[
            pl.BlockSpec((tc, GLA_HEAD_K), lambda b, h, i: (tok(b, h, i), COL_Q // GLA_HEAD_K + h)),
            pl.BlockSpec((tc, GLA_HEAD_K), lambda b, h, i: (tok(b, h, i), COL_K // GLA_HEAD_K + h)),
            pl.BlockSpec((tc, GLA_HEAD_V), lambda b, h, i: (tok(b, h, i), COL_V // GLA_HEAD_V + h)),
            pl.BlockSpec((tc, GLA_HEAD_V), lambda b, h, i: (tok(b, h, i), COL_R // GLA_HEAD_V + h)),
            pl.BlockSpec((tc, LANES), lambda b, h, i: (tok(b, h, i), COL_GLOW // LANES)),
            pl.BlockSpec((1, LANES, GLA_HEAD_K), lambda b, h, i: (h, 0, 0)),
            pl.BlockSpec((1, 1, GLA_HEAD_K), lambda b, h, i: (h, 0, 0)),
            pl.BlockSpec((1, GLA_HEAD_V), lambda b, h, i: (0, 0)),
            pl.BlockSpec(((len(GLA_LEVELS) + 1) * CHUNK, CHUNK), lambda b, h, i: (0, 0)),
        ],
        out_specs=pl.BlockSpec((tc, GLA_HEAD_V), lambda b, h, i: (tok(b, h, i), h)),
        out_shape=jax.ShapeDtypeStruct((t, GLA_DV), F32),
        scratch_shapes=[pltpu.VMEM((GLA_HEAD_K, GLA_HEAD_V), F32),
                        pltpu.VMEM((tc, GLA_HEAD_K), F32)],
        compiler_params=_cparams(("arbitrary", "arbitrary", "arbitrary")),
        name="gla",
    )(proj, proj, proj, proj, proj, wg_pad, bg, gn, _gla_select_matrix())


CONV_PAD = SUBLANES


def _ssd_kernel(z_ref, x_ref, bc_ref, dt_ref, cw_ref, cb_ref, dtb_ref, aneg_ref, dsk_ref, ng_ref, exp_ref,
                o_ref, xs_ref, conv_ref, dtx_ref, h_ref, yd_ref, *, tc):
    @pl.when(pl.program_id(1) == 0)
    def _():
        xs_ref[0:CONV_PAD, :] = jnp.zeros((CONV_PAD, SSD_CONV_DIM), F32)
        h_ref[...] = jnp.zeros_like(h_ref)

    xs_ref[CONV_PAD:CONV_PAD + tc, 0:SSD_DINNER] = x_ref[...]
    xs_ref[CONV_PAD:CONV_PAD + tc, SSD_DINNER:SSD_CONV_DIM] = bc_ref[...]
    conv = cb_ref[...] + cw_ref[0:1, :] * xs_ref[CONV_PAD - 3:CONV_PAD - 3 + tc, :]
    for kk in range(1, SSD_CONV):
        off = CONV_PAD - (SSD_CONV - 1) + kk
        conv = conv + cw_ref[kk:kk + 1, :] * xs_ref[off:off + tc, :]
    xs_ref[0:CONV_PAD, :] = xs_ref[tc:tc + CONV_PAD, :]
    conv_ref[...] = _silu(conv)
    dt = _softplus(dt_ref[...] + dtb_ref[...])
    dtx_ref[...] = _dot_exact(dt, exp_ref[...])

    row = lax.broadcasted_iota(jnp.int32, (CHUNK, CHUNK), 0)
    col = lax.broadcasted_iota(jnp.int32, (CHUNK, CHUNK), 1)
    causal = row >= col
    tril = causal.astype(F32)
    n0 = SSD_DINNER
    n1 = SSD_DINNER + SSD_GROUPS * SSD_STATE

    def chunk(c, carry):
        r0 = pl.multiple_of(c * CHUNK, CHUNK)
        dtc = dtx_ref[pl.ds(r0, CHUNK), :]
        cs = _dot_exact(tril, dtc * aneg_ref[...])
        x = conv_ref[pl.ds(r0, CHUNK), 0:n0]
        xdt = x * dtc
        cs_last = cs[CHUNK - 1:CHUNK, :]
        xw = (xdt * jnp.exp(cs_last - cs)).astype(BF16)
        xdt_b = xdt.astype(BF16)
        ecs = jnp.exp(cs)
        ecl = jnp.exp(cs_last)
        y_parts = []
        for g in range(SSD_GROUPS):
            lo = g * SSD_GROUP_W
            bg = conv_ref[pl.ds(r0, CHUNK), n0 + g * SSD_STATE:n0 + (g + 1) * SSD_STATE].astype(BF16)
            cg = conv_ref[pl.ds(r0, CHUNK), n1 + g * SSD_STATE:n1 + (g + 1) * SSD_STATE].astype(BF16)
            cb = _dot_nt(cg, bg)
            hg = h_ref[g]
            y_off = _dot(cg, hg.astype(BF16)) * ecs[:, lo:lo + SSD_GROUP_W]
            for hh in range(SSD_HPG):
                c0 = lo + hh * SSD_HEADDIM
                colv = cs[:, c0:c0 + SSD_HEADDIM]
                rowv = colv.T
                lm = jnp.exp(jnp.where(causal, colv - rowv, -jnp.inf))
                yd_ref[:, c0:c0 + SSD_HEADDIM] = _dot((cb * lm).astype(BF16), xdt_b[:, c0:c0 + SSD_HEADDIM])
            states = _dot_tn(bg, xw[:, lo:lo + SSD_GROUP_W])
            h_ref[g] = hg * ecl[:, lo:lo + SSD_GROUP_W] + states
            y_parts.append(y_off)
        y = yd_ref[...] + jnp.concatenate(y_parts, axis=1) + dsk_ref[...] * x
        y = y * _silu(z_ref[pl.ds(r0, CHUNK), :])
        outs = []
        for g in range(SSD_GROUPS):
            yg = y[:, g * SSD_GROUP_W:(g + 1) * SSD_GROUP_W]
            outs.append(yg * lax.rsqrt(jnp.mean(yg * yg, axis=-1, keepdims=True) + RMS_EPS))
        o_ref[pl.ds(r0, CHUNK), :] = jnp.concatenate(outs, axis=1) * ng_ref[...]
        return carry

    lax.fori_loop(0, tc // CHUNK, chunk, 0)


def _ssd(proj, cw, cb, dtb, aneg, dsk, ng, expand, bsz, seq, tc):
    nt = seq // tc
    t = bsz * seq
    kern = functools.partial(_ssd_kernel, tc=tc)
    full = lambda shape: pl.BlockSpec(shape, lambda b, i: tuple(0 for _ in shape))
    return pl.pallas_call(
        kern,
        grid=(bsz, nt),
        in_specs=[
            pl.BlockSpec((tc, SSD_DINNER), lambda b, i: (b * nt + i, COL_Z // SSD_DINNER)),
            pl.BlockSpec((tc, SSD_DINNER), lambda b, i: (b * nt + i, COL_XS // SSD_DINNER)),
            pl.BlockSpec((tc, SSD_BC_W), lambda b, i: (b * nt + i, COL_BC // SSD_BC_W)),
            pl.BlockSpec((tc, LANES), lambda b, i: (b * nt + i, COL_DT // LANES)),
            full((SSD_CONV, SSD_CONV_DIM)), full((1, SSD_CONV_DIM)), full((1, LANES)),
            full((1, SSD_DINNER)), full((1, SSD_DINNER)), full((1, SSD_DINNER)),
            full((LANES, SSD_DINNER)),
        ],
        out_specs=pl.BlockSpec((tc, SSD_DINNER), lambda b, i: (b * nt + i, 0)),
        out_shape=jax.ShapeDtypeStruct((t, SSD_DINNER), F32),
        scratch_shapes=[pltpu.VMEM((tc + CONV_PAD, SSD_CONV_DIM), F32),
                        pltpu.VMEM((tc, SSD_CONV_DIM), F32),
                        pltpu.VMEM((tc, SSD_DINNER), F32),
                        pltpu.VMEM((SSD_GROUPS, SSD_STATE, SSD_GROUP_W), F32),
                        pltpu.VMEM((CHUNK, SSD_DINNER), F32)],
        compiler_params=_cparams(("arbitrary", "arbitrary")),
        name="ssd",
    )(proj, proj, proj, proj, cw, cb, dtb, aneg, dsk, ng, expand)


def _outproj_kernel(o_ref, y_ref, x_ref, w_ref, g_ref, b_ref, h_ref):
    mix = (_dot(o_ref[...].astype(BF16), w_ref[0:GLA_DV, :])
           + _dot(y_ref[...].astype(BF16), w_ref[GLA_DV:GLA_DV + SSD_DINNER, :]))
    h_ref[...] = _layer_norm(DEEPNORM_ALPHA * x_ref[...] + mix, g_ref[...], b_ref[...])


def _outproj(o, y, x, w, g, b, tm):
    t = x.shape[0]
    tile = lambda w_: pl.BlockSpec((tm, w_), lambda i: (i, 0))
    full = lambda shape: pl.BlockSpec(shape, lambda i: tuple(0 for _ in shape))
    return pl.pallas_call(
        _outproj_kernel,
        grid=(t // tm,),
        in_specs=[tile(GLA_DV), tile(SSD_DINNER), tile(D_MODEL), full(w.shape),
                  full((1, D_MODEL)), full((1, D_MODEL))],
        out_specs=tile(D_MODEL),
        out_shape=jax.ShapeDtypeStruct((t, D_MODEL), F32),
        compiler_params=_cparams(("arbitrary",)),
        name="outproj_ln",
    )(o, y, x, w, g, b)


def _xattn_kernel(h_ref, k_ref, v_ref, wq_ref, wo_ref, g_ref, b_ref, o_ref):
    h = h_ref[...]
    q = _dot(h.astype(BF16), wq_ref[...]).astype(BF16)
    scale = MEM_HEAD_DIM ** -0.5
    heads = []
    for hd in range(MEM_HEADS):
        sl = slice(hd * MEM_HEAD_DIM, (hd + 1) * MEM_HEAD_DIM)
        s = _dot_nt(q[:, sl], k_ref[:, sl].astype(BF16)) * scale
        s = s - jnp.max(s, axis=-1, keepdims=True)
        e = jnp.exp(s)
        p = e / jnp.sum(e, axis=-1, keepdims=True)
        heads.append(_dot(p.astype(BF16), v_ref[:, sl].astype(BF16)))
    ca = _dot(jnp.concatenate(heads, axis=1).astype(BF16), wo_ref[...])
    o_ref[...] = _layer_norm(DEEPNORM_ALPHA * h + ca, g_ref[...], b_ref[...])


def _xattn(h, kmem, vmem, wq, wo, g, b, bsz, seq, mem_len, tm):
    nt = seq // tm
    t = bsz * seq
    full = lambda shape: pl.BlockSpec(shape, lambda bb, i: tuple(0 for _ in shape))
    return pl.pallas_call(
        _xattn_kernel,
        grid=(bsz, nt),
        in_specs=[pl.BlockSpec((tm, D_MODEL), lambda bb, i: (bb * nt + i, 0)),
                  pl.BlockSpec((mem_len, D_MODEL), lambda bb, i: (bb, 0)),
                  pl.BlockSpec((mem_len, D_MODEL), lambda bb, i: (bb, 0)),
                  full((D_MODEL, D_MODEL)), full((D_MODEL, D_MODEL)),
                  full((1, D_MODEL)), full((1, D_MODEL))],
        out_specs=pl.BlockSpec((tm, D_MODEL), lambda bb, i: (bb * nt + i, 0)),
        out_shape=jax.ShapeDtypeStruct((t, D_MODEL), F32),
        compiler_params=_cparams(("arbitrary", "arbitrary")),
        name="xattn_ln",
    )(h, kmem, vmem, wq, wo, g, b)


N_SEL = PEER_HEADS * PEER_TOPK
NEG_INF = float("-inf")

_CAND_GROUPS = ((0, 0, 8), (0, 8, 8), (1, 0, 8), (2, 0, 5), (3, 0, 4), (4, 0, 3), (5, 0, 2), (6, 0, 2), (7, 0, 2))


def _peer_select_kernel(h_ref, wq_ref, keys_ref, i1_ref, i2_ref, g_ref, hb_ref,
                        st_ref, it_ref, i1s_ref, i2s_ref, gs_ref, *, tt):
    hb = h_ref[...].astype(BF16)
    hb_ref[...] = hb
    q = _dot(hb, wq_ref[...]).astype(BF16)
    key_iota = lax.broadcasted_iota(jnp.int32, (PEER_N_KEYS, tt), 0).astype(F32)
    row8 = lax.broadcasted_iota(jnp.int32, (SUBLANES, tt), 0).astype(F32)
    row16 = lax.broadcasted_iota(jnp.int32, (PEER_TOPK, tt), 0).astype(F32)
    big = float(PEER_TOPK * PEER_TOPK)

    for head in range(PEER_HEADS):
        for p in range(2):
            c0 = (head * 2 + p) * PEER_HALF
            x = _dot_nt(keys_ref[p], q[:, c0:c0 + PEER_HALF])
            for kk in range(PEER_TOPK):
                m = jnp.max(x, axis=0, keepdims=True)
                idx = jnp.min(jnp.where(x == m, key_iota, float(PEER_N_KEYS)), axis=0, keepdims=True)
                x = jnp.where(key_iota == idx, NEG_INF, x)
                st_ref[p, kk:kk + 1, :] = m
                it_ref[p, kk:kk + 1, :] = idx
        s1 = st_ref[0]
        s2 = st_ref[1]
        vals, flats = [], []
        for (a, b0, cnt) in _CAND_GROUPS:
            v = s1[a:a + 1, :] + s2[b0:b0 + SUBLANES, :]
            vals.append(jnp.where(row8 < float(cnt), v, NEG_INF))
            flats.append(float(a * PEER_TOPK + b0) + row8)
        vals.append(s1[SUBLANES:PEER_TOPK, :] + s2[0:1, :])
        flats.append((float(SUBLANES) + row8) * float(PEER_TOPK))
        best_rows = []
        for kk in range(PEER_TOPK):
            m = functools.reduce(jnp.maximum, vals)
            m = jnp.max(m, axis=0, keepdims=True)
            cand = functools.reduce(jnp.minimum,
                                    [jnp.where(v == m, f, big) for v, f in zip(vals, flats)])
            cidx = jnp.min(cand, axis=0, keepdims=True)
            vals = [jnp.where(f == cidx, NEG_INF, v) for v, f in zip(vals, flats)]
            a_sel = jnp.floor(cidx * (1.0 / PEER_TOPK))
            b_sel = cidx - a_sel * float(PEER_TOPK)
            i1 = jnp.sum(jnp.where(row16 == a_sel, it_ref[0], 0.0), axis=0, keepdims=True)
            i2 = jnp.sum(jnp.where(row16 == b_sel, it_ref[1], 0.0), axis=0, keepdims=True)
            r = head * PEER_TOPK + kk
            i1s_ref[r:r + 1, :] = i1
            i2s_ref[r:r + 1, :] = i2
            best_rows.append(m)
        e = [jnp.exp(b - best_rows[0]) for b in best_rows]
        denom = functools.reduce(lambda u, w: u + w, e)
        for kk in range(PEER_TOPK):
            r = head * PEER_TOPK + kk
            gs_ref[r:r + 1, :] = e[kk] / denom

    i1_ref[...] = i1s_ref[...].T.astype(jnp.int32)
    i2_ref[...] = i2s_ref[...].T.astype(jnp.int32)
    g_ref[...] = gs_ref[...].T


def _peer_select(h, wq, keys, tt):
    t = h.shape[0]
    kern = functools.partial(_peer_select_kernel, tt=tt)
    tile = lambda w_: pl.BlockSpec((tt, w_), lambda i: (i, 0))
    full = lambda shape: pl.BlockSpec(shape, lambda i: tuple(0 for _ in shape))
    return pl.pallas_call(
        kern,
        grid=(t // tt,),
        in_specs=[tile(D_MODEL), full((D_MODEL, D_MODEL)), full((2, PEER_N_KEYS, PEER_HALF))],
        out_specs=[tile(N_SEL), tile(N_SEL), tile(N_SEL), tile(D_MODEL)],
        out_shape=[jax.ShapeDtypeStruct((t, N_SEL), jnp.int32),
                   jax.ShapeDtypeStruct((t, N_SEL), jnp.int32),
                   jax.ShapeDtypeStruct((t, N_SEL), F32),
                   jax.ShapeDtypeStruct((t, D_MODEL), BF16)],
        scratch_shapes=[pltpu.VMEM((2, PEER_TOPK, tt), F32),
                        pltpu.VMEM((2, PEER_TOPK, tt), F32),
                        pltpu.VMEM((N_SEL, tt), F32),
                        pltpu.VMEM((N_SEL, tt), F32),
                        pltpu.VMEM((N_SEL, tt), F32)],
        compiler_params=_cparams(("arbitrary",)),
        name="peer_select",
    )(h, wq, keys)


def _peer_act_kernel(hb_ref, u_ref, i1_ref, i2_ref, val_ref, *, eb):
    act = _dot_nt(hb_ref[...], u_ref[...])
    i1 = i1_ref[...]
    i2 = i2_ref[...]
    slabs = eb // PEER_N_KEYS
    slab0 = pl.program_id(0) * slabs
    acc = jnp.zeros(i1.shape, F32)
    for s in range(slabs):
        picked = jnp.take_along_axis(act[:, s * PEER_N_KEYS:(s + 1) * PEER_N_KEYS], i2, axis=1)
        acc = jnp.where(i1 == slab0 + s, picked, acc)
    val_ref[0] = acc


def _peer_act(hb, u, i1, i2, tt, eb):
    t = hb.shape[0]
    ne = u.shape[0] // eb
    kern = functools.partial(_peer_act_kernel, eb=eb)
    return pl.pallas_call(
        kern,
        grid=(ne, t // tt),
        in_specs=[pl.BlockSpec((tt, D_MODEL), lambda e, i: (i, 0)),
                  pl.BlockSpec((eb, D_MODEL), lambda e, i: (e, 0)),
                  pl.BlockSpec((tt, N_SEL), lambda e, i: (i, 0)),
                  pl.BlockSpec((tt, N_SEL), lambda e, i: (i, 0))],
        out_specs=pl.BlockSpec((1, tt, N_SEL), lambda e, i: (e, i, 0)),
        out_shape=jax.ShapeDtypeStruct((ne, t, N_SEL), F32),
        compiler_params=_cparams(("arbitrary", "arbitrary")),
        name="peer_act",
    )(hb, u, i1, i2)


def _peer_out_kernel(val_ref, g_ref, i1_ref, i2_ref, v_ref, h_ref, lg_ref, lb_ref, o_ref,
                     w_ref, wtok_ref, acc_ref, *, tt, eb, ne):
    j = pl.program_id(1)

    @pl.when(j == 0)
    def _():
        val = val_ref[0]
        for e in range(1, ne):
            val = val + val_ref[e]
        gelu = 0.5 * val * (1.0 + lax.erf(val * (2.0 ** -0.5)))
        w_ref[...] = g_ref[...] * gelu
        acc_ref[...] = jnp.zeros_like(acc_ref)
        sub = lax.broadcasted_iota(jnp.int32, (PEER_N_KEYS, N_SEL), 0)

        def tok(tk, carry):
            i1row = i1_ref[pl.ds(tk, 1), :]
            i2row = i2_ref[pl.ds(tk, 1), :]
            wrow = w_ref[pl.ds(tk, 1), :]
            p1 = jnp.where(sub == i1row, 1.0, 0.0).astype(BF16)
            p2w = jnp.where(sub == i2row, wrow, 0.0).astype(BF16)
            r0 = pl.multiple_of(tk * PEER_N_KEYS, PEER_N_KEYS)
            wtok_ref[pl.ds(r0, PEER_N_KEYS), :] = _dot_nt(p1, p2w)
            return carry

        lax.fori_loop(0, tt, tok, 0)

    slabs = eb // PEER_N_KEYS
    cols = []
    for s in range(slabs):
        slab = j * slabs + s
        cols.append(wtok_ref[pl.ds(slab, tt, stride=PEER_N_KEYS), :].astype(BF16))
    acc_ref[...] += _dot(jnp.concatenate(cols, axis=1), v_ref[...])

    @pl.when(j == ne - 1)
    def _():
        o_ref[...] = _layer_norm(DEEPNORM_ALPHA * h_ref[...] + acc_ref[...], lg_ref[...], lb_ref[...])


def _peer_out(val, gates, i1, i2, v, h, lg, lb, tt, eb):
    t = h.shape[0]
    ne = v.shape[0] // eb
    kern = functools.partial(_peer_out_kernel, tt=tt, eb=eb, ne=ne)
    tile = lambda w_: pl.BlockSpec((tt, w_), lambda i, j: (i, 0))
    full = lambda shape: pl.BlockSpec(shape, lambda i, j: tuple(0 for _ in shape))
    return pl.pallas_call(
        kern,
        grid=(t // tt, ne),
        in_specs=[pl.BlockSpec((ne, tt, N_SEL), lambda i, j: (0, i, 0)),
                  tile(N_SEL), tile(N_SEL), tile(N_SEL),
                  pl.BlockSpec((eb, D_MODEL), lambda i, j: (j, 0)),
                  tile(D_MODEL), full((1, D_MODEL)), full((1, D_MODEL))],
        out_specs=tile(D_MODEL),
        out_shape=jax.ShapeDtypeStruct((t, D_MODEL), F32),
        scratch_shapes=[pltpu.VMEM((tt, N_SEL), F32),
                        pltpu.VMEM((tt * PEER_N_KEYS, PEER_N_KEYS), F32),
                        pltpu.VMEM((tt, D_MODEL), F32)],
        compiler_params=_cparams(("arbitrary", "arbitrary")),
        name="peer_out",
    )(val, gates, i1, i2, v, h, lg, lb)


def _pad_lanes(a, width):
    return jnp.pad(a, ((0, 0), (0, width - a.shape[1])))


def _regroup_w_in(w_in):
    sizes = (GLA_DK, GLA_DK, GLA_DV, GLA_DV, GLA_GATE_RANK, SSD_DINNER, SSD_CONV_DIM, SSD_HEADS)
    offs = [0]
    for s in sizes:
        offs.append(offs[-1] + s)
    q, k, v, r, glow, z, xbc, dt = (w_in[:, offs[i]:offs[i + 1]] for i in range(len(sizes)))
    return jnp.concatenate(
        [z, xbc[:, :SSD_DINNER], v, r, q, k, xbc[:, SSD_DINNER:], _pad_lanes(glow, LANES), _pad_lanes(dt, LANES)],
        axis=1).astype(BF16)


def _layer(h, mem, w_in, gla_wg2, gla_bg, gla_norm_g, conv_w, conv_b, dt_bias, a_log, d_skip, ssd_norm_g,
           w_out, ln1_g, ln1_b, ca_wq, ca_wk, ca_wv, ca_wo, ln2_g, ln2_b, peer_wq, peer_keys, peer_u,
           peer_v, ln3_g, ln3_b, bsz, seq):
    t = bsz * seq
    tm = min(512, seq)
    tc = min(512, seq)
    row = lambda a: a.reshape(1, -1)

    proj = _matmul(h, _regroup_w_in(w_in), tm, 512)
    wg_pad = jnp.pad(gla_wg2, ((0, LANES - GLA_GATE_RANK), (0, 0)))
    wg_pad = wg_pad.reshape(LANES, GLA_HEADS, GLA_HEAD_K).transpose(1, 0, 2).astype(BF16)
    o = _gla(proj, wg_pad, gla_bg.reshape(GLA_HEADS, 1, GLA_HEAD_K), row(gla_norm_g), bsz, seq, tc)
    rep = lambda a: jnp.repeat(a.astype(F32), SSD_HEADDIM).reshape(1, SSD_DINNER)
    expand = (jnp.arange(LANES)[:, None] == (jnp.arange(SSD_DINNER)[None, :] // SSD_HEADDIM)).astype(F32)
    y = _ssd(proj, conv_w, row(conv_b), _pad_lanes(row(dt_bias), LANES), rep(-jnp.exp(a_log.astype(F32))),
             rep(d_skip), row(ssd_norm_g), expand, bsz, seq, tc)
    h1 = _outproj(o, y, h, w_out.astype(BF16), row(ln1_g), row(ln1_b), tm)

    mem_len = mem.shape[1]
    mem2 = mem.reshape(bsz * mem_len, D_MODEL)
    kv = _matmul(mem2, jnp.concatenate([ca_wk, ca_wv], axis=1).astype(BF16), min(512, bsz * mem_len), 512)
    h2 = _xattn(h1, kv[:, :D_MODEL], kv[:, D_MODEL:], ca_wq.astype(BF16), ca_wo.astype(BF16),
                row(ln2_g), row(ln2_b), bsz, seq, mem_len, tm)

    tt = min(256, t)
    i1, i2, gates, hb = _peer_select(h2, peer_wq.astype(BF16), peer_keys.astype(BF16), tt)
    eb = 2048
    val = _peer_act(hb, peer_u.astype(BF16), i1, i2, min(512, t), eb)
    return _peer_out(val, gates, i1, i2, peer_v.astype(BF16), h2, row(ln3_g), row(ln3_b), tt, eb)


def kernel(x, mem, w_in, gla_wg2, gla_bg, gla_norm_g, ssd_conv_w, ssd_conv_b, ssd_dt_bias, ssd_a_log, ssd_d,
           ssd_norm_g, w_out, ln1_g, ln1_b, ca_wq, ca_wk, ca_wv, ca_wo, ln2_g, ln2_b, peer_wq, peer_keys,
           peer_u, peer_v, ln3_g, ln3_b):
    bsz, seq, _ = x.shape
    h = x.reshape(bsz * seq, D_MODEL)
    for l in range(w_in.shape[0]):
        h = _layer(h, mem, w_in[l], gla_wg2[l], gla_bg[l], gla_norm_g[l], ssd_conv_w[l], ssd_conv_b[l],
                   ssd_dt_bias[l], ssd_a_log[l], ssd_d[l], ssd_norm_g[l], w_out[l], ln1_g[l], ln1_b[l],
                   ca_wq[l], ca_wk[l], ca_wv[l], ca_wo[l], ln2_g[l], ln2_b[l], peer_wq[l], peer_keys[l],
                   peer_u[l], peer_v[l], ln3_g[l], ln3_b[l], bsz, seq)
    return h.reshape(bsz, seq, D_MODEL)
```

```python
import functools
import math

import jax
import jax.numpy as jnp
from jax import lax
from jax.experimental import pallas as pl
from jax.experimental.pallas import tpu as pltpu

F32 = jnp.float32
BF16 = jnp.bfloat16
HIGHEST = lax.Precision.HIGHEST

D_MODEL = 1024
GLA_HEADS = 4
GLA_HEAD_K = 128
GLA_HEAD_V = 256
GLA_DK = GLA_HEADS * GLA_HEAD_K
GLA_DV = GLA_HEADS * GLA_HEAD_V
GLA_GATE_RANK = 16
GLA_GATE_TAU = 16.0
CHUNK = 64
SSD_DINNER = 1024
SSD_HEADDIM = 64
SSD_HEADS = 16
SSD_GROUPS = 2
SSD_HPG = SSD_HEADS // SSD_GROUPS
SSD_STATE = 64
SSD_CONV = 4
SSD_CONV_DIM = SSD_DINNER + 2 * SSD_GROUPS * SSD_STATE
SSD_GROUP_W = SSD_DINNER // SSD_GROUPS
MEM_HEADS = 4
MEM_HEAD_DIM = 256
PEER_HEADS = 8
PEER_N_KEYS = 128
PEER_HALF = 64
PEER_TOPK = 16
DEPTH = 1
DEEPNORM_ALPHA = (2.0 * DEPTH) ** 0.25
LN_EPS = 1e-5
RMS_EPS = 1e-6

LANES = 128
SUBLANES = 8
VMEM_LIMIT = 56 * 1024 * 1024

SSD_BC_W = SSD_CONV_DIM - SSD_DINNER
COL_Z = 0
COL_XS = COL_Z + SSD_DINNER
COL_V = COL_XS + SSD_DINNER
COL_R = COL_V + GLA_DV
COL_Q = COL_R + GLA_DV
COL_K = COL_Q + GLA_DK
COL_BC = COL_K + GLA_DK
COL_GLOW = COL_BC + SSD_BC_W
COL_DT = COL_GLOW + LANES
PROJ_W = COL_DT + LANES


def _cparams(sem):
    return pltpu.CompilerParams(dimension_semantics=sem, vmem_limit_bytes=VMEM_LIMIT)


def _dot(a, b):
    return jnp.dot(a, b, preferred_element_type=F32)


def _dot_nt(a, b):
    return lax.dot_general(a, b, (((1,), (1,)), ((), ())), preferred_element_type=F32)


def _dot_tn(a, b):
    return lax.dot_general(a, b, (((0,), (0,)), ((), ())), preferred_element_type=F32)


def _dot_exact(a, b):
    return jnp.dot(a, b, preferred_element_type=F32, precision=HIGHEST)


def _silu(x):
    return x / (1.0 + jnp.exp(-x))


def _softplus(x):
    return jnp.maximum(x, 0.0) + jnp.log1p(jnp.exp(-jnp.abs(x)))


def _layer_norm(x, g, b):
    mu = jnp.mean(x, axis=-1, keepdims=True)
    xc = x - mu
    var = jnp.mean(xc * xc, axis=-1, keepdims=True)
    return xc * lax.rsqrt(var + LN_EPS) * g + b


def _matmul_kernel(x_ref, w_ref, o_ref):
    o_ref[...] = _dot(x_ref[...].astype(BF16), w_ref[...])


def _matmul(x, w, tm, tn):
    m, k = x.shape
    n = w.shape[1]
    return pl.pallas_call(
        _matmul_kernel,
        grid=(m // tm, n // tn),
        in_specs=[pl.BlockSpec((tm, k), lambda i, j: (i, 0)),
                  pl.BlockSpec((k, tn), lambda i, j: (0, j))],
        out_specs=pl.BlockSpec((tm, tn), lambda i, j: (i, j)),
        out_shape=jax.ShapeDtypeStruct((m, n), F32),
        compiler_params=_cparams(("arbitrary", "arbitrary")),
        name="matmul",
    )(x, w)


GLA_LEVELS = (32, 16, 8, 4, 2, 1)


def _gla_select_matrix():
    t = jnp.arange(CHUNK)
    tril = (t[:, None] >= t[None, :]).astype(F32)
    mats = [tril]
    for c in GLA_LEVELS:
        mid = (t // (2 * c)) * (2 * c) + c
        mats.append(tril[mid])
    return jnp.concatenate(mats, axis=0)


def _dot_sel_exact(sel, x):
    hi = x.astype(BF16)
    r1 = x - hi.astype(F32)
    mid = r1.astype(BF16)
    lo = (r1 - mid.astype(F32)).astype(BF16)
    return _dot(sel, hi) + _dot(sel, mid) + _dot(sel, lo)


def _gla_kernel(q_ref, k_ref, v_ref, r_ref, gl_ref, wg_ref, bg_ref, gn_ref, sel_ref, o_ref,
                state_ref, loga_ref, *, tc):
    @pl.when(pl.program_id(1) == 0)
    def _():
        state_ref[...] = jnp.zeros_like(state_ref)

    gate_pre = _dot(gl_ref[...].astype(BF16), wg_ref[...]) + bg_ref[...]
    loga_ref[...] = -_softplus(-gate_pre) * (1.0 / GLA_GATE_TAU)

    row = lax.broadcasted_iota(jnp.int32, (CHUNK, CHUNK), 0)
    col = lax.broadcasted_iota(jnp.int32, (CHUNK, CHUNK), 1)
    masks = []
    for cs in GLA_LEVELS:
        same_blk = (row // (2 * cs)) == (col // (2 * cs))
        masks.append(same_blk & ((row % (2 * cs)) >= cs) & ((col % (2 * cs)) < cs))
    diag = row == col
    scale = GLA_HEAD_K ** -0.5

    def chunk(c, carry):
        r0 = pl.multiple_of(c * CHUNK, CHUNK)
        rows = pl.ds(r0, CHUNK)
        ball_all = _dot_sel_exact(sel_ref[...], loga_ref[rows, :])
        for h in range(GLA_HEADS):
            ks = slice(h * GLA_HEAD_K, (h + 1) * GLA_HEAD_K)
            vs = slice(h * GLA_HEAD_V, (h + 1) * GLA_HEAD_V)
            ball = ball_all[:, ks]
            b = ball[0:CHUNK]
            q = q_ref[rows, ks] * scale
            k = k_ref[rows, ks]
            v = v_ref[rows, vs].astype(BF16)
            scores = jnp.where(diag, _dot_nt(q.astype(BF16), k.astype(BF16)), 0.0)
            for li in range(len(GLA_LEVELS)):
                d = b - ball[(li + 1) * CHUNK:(li + 2) * CHUNK]
                qe = q * jnp.exp(jnp.minimum(d, 0.0))
                ke = k * jnp.exp(jnp.minimum(-d, 0.0))
                scores = jnp.where(masks[li], _dot_nt(qe.astype(BF16), ke.astype(BF16)), scores)
            state = state_ref[h]
            o = _dot(scores.astype(BF16), v) + _dot((q * jnp.exp(b)).astype(BF16), state.astype(BF16))
            b_last = b[CHUNK - 1:CHUNK, :]
            kd = k * jnp.exp(b_last - b)
            decay = jnp.exp(jnp.broadcast_to(b_last, (GLA_HEAD_K, GLA_HEAD_K)).T)
            state_ref[h] = (jnp.concatenate([decay] * (GLA_HEAD_V // GLA_HEAD_K), axis=1) * state
                            + _dot_tn(kd.astype(BF16), v))
            o = o * lax.rsqrt(jnp.mean(o * o, axis=-1, keepdims=True) + RMS_EPS)
            o_ref[rows, vs] = o * gn_ref[...] * _silu(r_ref[rows, vs])
        return carry

    lax.fori_loop(0, tc // CHUNK, chunk, 0)


def _gla(proj, wg_pad, bg, gn, bsz, seq, tc):
    nt = seq // tc
    t = bsz * seq
    kern = functools.partial(_gla_kernel, tc=tc)
    full = lambda shape: pl.BlockSpec(shape, lambda b, i: tuple(0 for _ in shape))
    return pl.pallas_call(
        kern,
        grid=(bsz, nt),
        in_specs=[
            pl.BlockSpec((tc, GLA_DK), lambda b, i: (b * nt + i, COL_Q // GLA_DK)),
            pl.BlockSpec((tc, GLA_DK), lambda b, i: (b * nt + i, COL_K // GLA_DK)),
            pl.BlockSpec((tc, GLA_DV), lambda b, i: (b * nt + i, COL_V // GLA_DV)),
            pl.BlockSpec((tc, GLA_DV), lambda b, i: (b * nt + i, COL_R // GLA_DV)),
            pl.BlockSpec((tc, LANES), lambda b, i: (b * nt + i, COL_GLOW // LANES)),
            full((LANES, GLA_DK)), full((1, GLA_DK)), full((1, GLA_HEAD_V)),
            full(((len(GLA_LEVELS) + 1) * CHUNK, CHUNK)),
        ],
        out_specs=pl.BlockSpec((tc, GLA_DV), lambda b, i: (b * nt + i, 0)),
        out_shape=jax.ShapeDtypeStruct((t, GLA_DV), F32),
        scratch_shapes=[pltpu.VMEM((GLA_HEADS, GLA_HEAD_K, GLA_HEAD_V), F32),
                        pltpu.VMEM((tc, GLA_DK), F32)],
        compiler_params=_cparams(("arbitrary", "arbitrary")),
        name="gla",
    )(proj, proj, proj, proj, proj, wg_pad, bg, gn, _gla_select_matrix().astype(BF16))


CONV_PAD = SUBLANES


def _ssd_kernel(z_ref, x_ref, bc_ref, dt_ref, cw_ref, cb_ref, dtb_ref, aneg_ref, dsk_ref, ng_ref, exp_ref,
                o_ref, xs_ref, conv_ref, dtx_ref, h_ref, yd_ref, *, tc):
    @pl.when(pl.program_id(1) == 0)
    def _():
        xs_ref[0:CONV_PAD, :] = jnp.zeros((CONV_PAD, SSD_CONV_DIM), F32)
        h_ref[...] = jnp.zeros_like(h_ref)

    xs_ref[CONV_PAD:CONV_PAD + tc, 0:SSD_DINNER] = x_ref[...]
    xs_ref[CONV_PAD:CONV_PAD + tc, SSD_DINNER:SSD_CONV_DIM] = bc_ref[...]
    conv = cb_ref[...] + cw_ref[0:1, :] * xs_ref[CONV_PAD - 3:CONV_PAD - 3 + tc, :]
    for kk in range(1, SSD_CONV):
        off = CONV_PAD - (SSD_CONV - 1) + kk
        conv = conv + cw_ref[kk:kk + 1, :] * xs_ref[off:off + tc, :]
    xs_ref[0:CONV_PAD, :] = xs_ref[tc:tc + CONV_PAD, :]
    conv_ref[...] = _silu(conv)
    dt = _softplus(dt_ref[...] + dtb_ref[...])
    dtx_ref[...] = _dot_exact(dt, exp_ref[...])

    row = lax.broadcasted_iota(jnp.int32, (CHUNK, CHUNK), 0)
    col = lax.broadcasted_iota(jnp.int32, (CHUNK, CHUNK), 1)
    causal = row >= col
    tril = causal.astype(F32)
    n0 = SSD_DINNER
    n1 = SSD_DINNER + SSD_GROUPS * SSD_STATE

    def chunk(c, carry):
        r0 = pl.multiple_of(c * CHUNK, CHUNK)
        dtc = dtx_ref[pl.ds(r0, CHUNK), :]
        cs = _dot_exact(tril, dtc * aneg_ref[...])
        x = conv_ref[pl.ds(r0, CHUNK), 0:n0]
        xdt = x * dtc
        cs_last = cs[CHUNK - 1:CHUNK, :]
        xw = (xdt * jnp.exp(cs_last - cs)).astype(BF16)
        xdt_b = xdt.astype(BF16)
        ecs = jnp.exp(cs)
        ecl = jnp.exp(cs_last)
        y_parts = []
        for g in range(SSD_GROUPS):
            lo = g * SSD_GROUP_W
            bg = conv_ref[pl.ds(r0, CHUNK), n0 + g * SSD_STATE:n0 + (g + 1) * SSD_STATE].astype(BF16)
            cg = conv_ref[pl.ds(r0, CHUNK), n1 + g * SSD_STATE:n1 + (g + 1) * SSD_STATE].astype(BF16)
            cb = _dot_nt(cg, bg)
            hg = h_ref[g]
            y_off = _dot(cg, hg.astype(BF16)) * ecs[:, lo:lo + SSD_GROUP_W]
            for hh in range(SSD_HPG):
                c0 = lo + hh * SSD_HEADDIM
                colv = cs[:, c0:c0 + SSD_HEADDIM]
                rowv = colv.T
                lm = jnp.exp(jnp.where(causal, colv - rowv, -jnp.inf))
                yd_ref[:, c0:c0 + SSD_HEADDIM] = _dot((cb * lm).astype(BF16), xdt_b[:, c0:c0 + SSD_HEADDIM])
            states = _dot_tn(bg, xw[:, lo:lo + SSD_GROUP_W])
            h_ref[g] = hg * ecl[:, lo:lo + SSD_GROUP_W] + states
            y_parts.append(y_off)
        y = yd_ref[...] + jnp.concatenate(y_parts, axis=1) + dsk_ref[...] * x
        y = y * _silu(z_ref[pl.ds(r0, CHUNK), :])
        outs = []
        for g in range(SSD_GROUPS):
            yg = y[:, g * SSD_GROUP_W:(g + 1) * SSD_GROUP_W]
            outs.append(yg * lax.rsqrt(jnp.mean(yg * yg, axis=-1, keepdims=True) + RMS_EPS))
        o_ref[pl.ds(r0, CHUNK), :] = jnp.concatenate(outs, axis=1) * ng_ref[...]
        return carry

    lax.fori_loop(0, tc // CHUNK, chunk, 0)


def _ssd(proj, cw, cb, dtb, aneg, dsk, ng, expand, bsz, seq, tc):
    nt = seq // tc
    t = bsz * seq
    kern = functools.partial(_ssd_kernel, tc=tc)
    full = lambda shape: pl.BlockSpec(shape, lambda b, i: tuple(0 for _ in shape))
    return pl.pallas_call(
        kern,
        grid=(bsz, nt),
        in_specs=[
            pl.BlockSpec((tc, SSD_DINNER), lambda b, i: (b * nt + i, COL_Z // SSD_DINNER)),
            pl.BlockSpec((tc, SSD_DINNER), lambda b, i: (b * nt + i, COL_XS // SSD_DINNER)),
            pl.BlockSpec((tc, SSD_BC_W), lambda b, i: (b * nt + i, COL_BC // SSD_BC_W)),
            pl.BlockSpec((tc, LANES), lambda b, i: (b * nt + i, COL_DT // LANES)),
            full((SSD_CONV, SSD_CONV_DIM)), full((1, SSD_CONV_DIM)), full((1, LANES)),
            full((1, SSD_DINNER)), full((1, SSD_DINNER)), full((1, SSD_DINNER)),
            full((LANES, SSD_DINNER)),
        ],
        out_specs=pl.BlockSpec((tc, SSD_DINNER), lambda b, i: (b * nt + i, 0)),
        out_shape=jax.ShapeDtypeStruct((t, SSD_DINNER), F32),
        scratch_shapes=[pltpu.VMEM((tc + CONV_PAD, SSD_CONV_DIM), F32),
                        pltpu.VMEM((tc, SSD_CONV_DIM), F32),
                        pltpu.VMEM((tc, SSD_DINNER), F32),
                        pltpu.VMEM((SSD_GROUPS, SSD_STATE, SSD_GROUP_W), F32),
                        pltpu.VMEM((CHUNK, SSD_DINNER), F32)],
        compiler_params=_cparams(("arbitrary", "arbitrary")),
        name="ssd",
    )(proj, proj, proj, proj, cw, cb, dtb, aneg, dsk, ng, expand)


def _outproj_kernel(o_ref, y_ref, x_ref, w_ref, g_ref, b_ref, h_ref):
    mix = (_dot(o_ref[...].astype(BF16), w_ref[0:GLA_DV, :])
           + _dot(y_ref[...].astype(BF16), w_ref[GLA_DV:GLA_DV + SSD_DINNER, :]))
    h_ref[...] = _layer_norm(DEEPNORM_ALPHA * x_ref[...] + mix, g_ref[...], b_ref[...])


def _outproj(o, y, x, w, g, b, tm):
    t = x.shape[0]
    tile = lambda w_: pl.BlockSpec((tm, w_), lambda i: (i, 0))
    full = lambda shape: pl.BlockSpec(shape, lambda i: tuple(0 for _ in shape))
    return pl.pallas_call(
        _outproj_kernel,
        grid=(t // tm,),
        in_specs=[tile(GLA_DV), tile(SSD_DINNER), tile(D_MODEL), full(w.shape),
                  full((1, D_MODEL)), full((1, D_MODEL))],
        out_specs=tile(D_MODEL),
        out_shape=jax.ShapeDtypeStruct((t, D_MODEL), F32),
        compiler_params=_cparams(("arbitrary",)),
        name="outproj_ln",
    )(o, y, x, w, g, b)


def _xattn_kernel(h_ref, k_ref, v_ref, wq_ref, wo_ref, g_ref, b_ref, o_ref):
    h = h_ref[...]
    q = _dot(h.astype(BF16), wq_ref[...]).astype(BF16)
    scale = MEM_HEAD_DIM ** -0.5
    heads = []
    for hd in range(MEM_HEADS):
        sl = slice(hd * MEM_HEAD_DIM, (hd + 1) * MEM_HEAD_DIM)
        s = _dot_nt(q[:, sl], k_ref[:, sl].astype(BF16)) * scale
        s = s - jnp.max(s, axis=-1, keepdims=True)
        e = jnp.exp(s)
        p = e / jnp.sum(e, axis=-1, keepdims=True)
        heads.append(_dot(p.astype(BF16), v_ref[:, sl].astype(BF16)))
    ca = _dot(jnp.concatenate(heads, axis=1).astype(BF16), wo_ref[...])
    o_ref[...] = _layer_norm(DEEPNORM_ALPHA * h + ca, g_ref[...], b_ref[...])


def _xattn(h, kmem, vmem, wq, wo, g, b, bsz, seq, mem_len, tm):
    nt = seq // tm
    t = bsz * seq
    full = lambda shape: pl.BlockSpec(shape, lambda bb, i: tuple(0 for _ in shape))
    return pl.pallas_call(
        _xattn_kernel,
        grid=(bsz, nt),
        in_specs=[pl.BlockSpec((tm, D_MODEL), lambda bb, i: (bb * nt + i, 0)),
                  pl.BlockSpec((mem_len, D_MODEL), lambda bb, i: (bb, 0)),
                  pl.BlockSpec((mem_len, D_MODEL), lambda bb, i: (bb, 0)),
                  full((D_MODEL, D_MODEL)), full((D_MODEL, D_MODEL)),
                  full((1, D_MODEL)), full((1, D_MODEL))],
        out_specs=pl.BlockSpec((tm, D_MODEL), lambda bb, i: (bb * nt + i, 0)),
        out_shape=jax.ShapeDtypeStruct((t, D_MODEL), F32),
        compiler_params=_cparams(("arbitrary", "arbitrary")),
        name="xattn_ln",
    )(h, kmem, vmem, wq, wo, g, b)


N_SEL = PEER_HEADS * PEER_TOPK
TOK_GROUP = 16
NEG_INF = float("-inf")

_CAND_GROUPS = ((0, 0, 8), (0, 8, 8), (1, 0, 8), (2, 0, 5), (3, 0, 4), (4, 0, 3), (5, 0, 2), (6, 0, 2), (7, 0, 2))


def _peer_select_kernel(h_ref, wq_ref, keys_ref, i1_ref, i2_ref, g_ref, hb_ref,
                        st_ref, it_ref, i1s_ref, i2s_ref, gs_ref, *, tt):
    hb = h_ref[...].astype(BF16)
    hb_ref[...] = hb
    q = _dot(hb, wq_ref[...]).astype(BF16)
    key_iota = lax.broadcasted_iota(jnp.int32, (PEER_N_KEYS, tt), 0).astype(F32)
    row8 = lax.broadcasted_iota(jnp.int32, (SUBLANES, tt), 0).astype(F32)
    row16 = lax.broadcasted_iota(jnp.int32, (PEER_TOPK, tt), 0).astype(F32)
    big = float(PEER_TOPK * PEER_TOPK)

    for head in range(PEER_HEADS):
        for p in range(2):
            c0 = (head * 2 + p) * PEER_HALF
            x = _dot_nt(keys_ref[p], q[:, c0:c0 + PEER_HALF])
            for kk in range(PEER_TOPK):
                m = jnp.max(x, axis=0, keepdims=True)
                idx = jnp.min(jnp.where(x == m, key_iota, float(PEER_N_KEYS)), axis=0, keepdims=True)
                x = jnp.where(key_iota == idx, NEG_INF, x)
                st_ref[p, kk:kk + 1, :] = m
                it_ref[p, kk:kk + 1, :] = idx
        s1 = st_ref[0]
        s2 = st_ref[1]
        vals, flats = [], []
        for (a, b0, cnt) in _CAND_GROUPS:
            v = s1[a:a + 1, :] + s2[b0:b0 + SUBLANES, :]
            vals.append(jnp.where(row8 < float(cnt), v, NEG_INF))
            flats.append(float(a * PEER_TOPK + b0) + row8)
        vals.append(s1[SUBLANES:PEER_TOPK, :] + s2[0:1, :])
        flats.append((float(SUBLANES) + row8) * float(PEER_TOPK))
        best_rows = []
        for kk in range(PEER_TOPK):
            m = functools.reduce(jnp.maximum, vals)
            m = jnp.max(m, axis=0, keepdims=True)
            cand = functools.reduce(jnp.minimum,
                                    [jnp.where(v == m, f, big) for v, f in zip(vals, flats)])
            cidx = jnp.min(cand, axis=0, keepdims=True)
            vals = [jnp.where(f == cidx, NEG_INF, v) for v, f in zip(vals, flats)]
            a_sel = jnp.floor(cidx * (1.0 / PEER_TOPK))
            b_sel = cidx - a_sel * float(PEER_TOPK)
            i1 = jnp.sum(jnp.where(row16 == a_sel, it_ref[0], 0.0), axis=0, keepdims=True)
            i2 = jnp.sum(jnp.where(row16 == b_sel, it_ref[1], 0.0), axis=0, keepdims=True)
            r = head * PEER_TOPK + kk
            i1s_ref[r:r + 1, :] = i1
            i2s_ref[r:r + 1, :] = i2
            best_rows.append(m)
        e = [jnp.exp(b - best_rows[0]) for b in best_rows]
        denom = functools.reduce(lambda u, w: u + w, e)
        for kk in range(PEER_TOPK):
            r = head * PEER_TOPK + kk
            gs_ref[r:r + 1, :] = e[kk] / denom

    i1_ref[...] = i1s_ref[...].T.astype(jnp.int32)
    i2_ref[...] = i2s_ref[...].T.astype(jnp.int32)
    g_ref[...] = gs_ref[...].T


def _peer_select(h, wq, keys, tt):
    t = h.shape[0]
    kern = functools.partial(_peer_select_kernel, tt=tt)
    tile = lambda w_: pl.BlockSpec((tt, w_), lambda i: (i, 0))
    full = lambda shape: pl.BlockSpec(shape, lambda i: tuple(0 for _ in shape))
    return pl.pallas_call(
        kern,
        grid=(t // tt,),
        in_specs=[tile(D_MODEL), full((D_MODEL, D_MODEL)), full((2, PEER_N_KEYS, PEER_HALF))],
        out_specs=[tile(N_SEL), tile(N_SEL), tile(N_SEL), tile(D_MODEL)],
        out_shape=[jax.ShapeDtypeStruct((t, N_SEL), jnp.int32),
                   jax.ShapeDtypeStruct((t, N_SEL), jnp.int32),
                   jax.ShapeDtypeStruct((t, N_SEL), F32),
                   jax.ShapeDtypeStruct((t, D_MODEL), BF16)],
        scratch_shapes=[pltpu.VMEM((2, PEER_TOPK, tt), F32),
                        pltpu.VMEM((2, PEER_TOPK, tt), F32),
                        pltpu.VMEM((N_SEL, tt), F32),
                        pltpu.VMEM((N_SEL, tt), F32),
                        pltpu.VMEM((N_SEL, tt), F32)],
        compiler_params=_cparams(("arbitrary",)),
        name="peer_select",
    )(h, wq, keys)


def _peer_act_kernel(hb_ref, u_ref, i1_ref, i2_ref, val_ref, *, eb):
    act = _dot_nt(hb_ref[...], u_ref[...])
    i1 = i1_ref[...]
    i2 = i2_ref[...]
    slabs = eb // PEER_N_KEYS
    slab0 = pl.program_id(0) * slabs
    acc = jnp.zeros(i1.shape, F32)
    for s in range(slabs):
        picked = jnp.take_along_axis(act[:, s * PEER_N_KEYS:(s + 1) * PEER_N_KEYS], i2, axis=1)
        acc = jnp.where(i1 == slab0 + s, picked, acc)
    val_ref[0] = acc


def _peer_act(hb, u, i1, i2, tt, eb):
    t = hb.shape[0]
    ne = u.shape[0] // eb
    kern = functools.partial(_peer_act_kernel, eb=eb)
    return pl.pallas_call(
        kern,
        grid=(ne, t // tt),
        in_specs=[pl.BlockSpec((tt, D_MODEL), lambda e, i: (i, 0)),
                  pl.BlockSpec((eb, D_MODEL), lambda e, i: (e, 0)),
                  pl.BlockSpec((tt, N_SEL), lambda e, i: (i, 0)),
                  pl.BlockSpec((tt, N_SEL), lambda e, i: (i, 0))],
        out_specs=pl.BlockSpec((1, tt, N_SEL), lambda e, i: (e, i, 0)),
        out_shape=jax.ShapeDtypeStruct((ne, t, N_SEL), F32),
        compiler_params=_cparams(("arbitrary", "arbitrary")),
        name="peer_act",
    )(hb, u, i1, i2)


def _peer_out_kernel(val_ref, g_ref, i1_ref, i2_ref, v_ref, h_ref, lg_ref, lb_ref, o_ref,
                     w_ref, wslab_ref, acc_ref, *, tt, eb, ne):
    j = pl.program_id(1)

    @pl.when(j == 0)
    def _():
        val = val_ref[0]
        for e in range(1, ne):
            val = val + val_ref[e]
        gelu = 0.5 * val * (1.0 + lax.erf(val * (2.0 ** -0.5)))
        w_ref[...] = g_ref[...] * gelu
        acc_ref[...] = jnp.zeros_like(acc_ref)
        sub = lax.broadcasted_iota(jnp.int32, (PEER_N_KEYS, N_SEL), 0)

        def tok_group(gi, carry):
            t0 = pl.multiple_of(gi * TOK_GROUP, TOK_GROUP)
            mats = []
            for u in range(TOK_GROUP):
                i1row = i1_ref[pl.ds(t0 + u, 1), :]
                i2row = i2_ref[pl.ds(t0 + u, 1), :]
                wrow = w_ref[pl.ds(t0 + u, 1), :]
                p1 = jnp.where(sub == i1row, 1.0, 0.0).astype(BF16)
                p2w = jnp.where(sub == i2row, wrow, 0.0).astype(BF16)
                mats.append(_dot_nt(p1, p2w))
            g = jnp.swapaxes(jnp.stack(mats, axis=0), 0, 1)
            wslab_ref[:, pl.ds(t0, TOK_GROUP), :] = g.astype(BF16)
            return carry

        lax.fori_loop(0, tt // TOK_GROUP, tok_group, 0)

    slabs = eb // PEER_N_KEYS
    cols = [wslab_ref[j * slabs + s] for s in range(slabs)]
    acc_ref[...] += _dot(jnp.concatenate(cols, axis=1), v_ref[...])

    @pl.when(j == ne - 1)
    def _():
        o_ref[...] = _layer_norm(DEEPNORM_ALPHA * h_ref[...] + acc_ref[...], lg_ref[...], lb_ref[...])


def _peer_out(val, gates, i1, i2, v, h, lg, lb, tt, eb):
    t = h.shape[0]
    ne = v.shape[0] // eb
    kern = functools.partial(_peer_out_kernel, tt=tt, eb=eb, ne=ne)
    tile = lambda w_: pl.BlockSpec((tt, w_), lambda i, j: (i, 0))
    full = lambda shape: pl.BlockSpec(shape, lambda i, j: tuple(0 for _ in shape))
    return pl.pallas_call(
        kern,
        grid=(t // tt, ne),
        in_specs=[pl.BlockSpec((ne, tt, N_SEL), lambda i, j: (0, i, 0)),
                  tile(N_SEL), tile(N_SEL), tile(N_SEL),
                  pl.BlockSpec((eb, D_MODEL), lambda i, j: (j, 0)),
                  tile(D_MODEL), full((1, D_MODEL)), full((1, D_MODEL))],
        out_specs=tile(D_MODEL),
        out_shape=jax.ShapeDtypeStruct((t, D_MODEL), F32),
        scratch_shapes=[pltpu.VMEM((tt, N_SEL), F32),
                        pltpu.VMEM((PEER_N_KEYS, tt, PEER_N_KEYS), BF16),
                        pltpu.VMEM((tt, D_MODEL), F32)],
        compiler_params=_cparams(("arbitrary", "arbitrary")),
        name="peer_out",
    )(val, gates, i1, i2, v, h, lg, lb)


def _pad_lanes(a, width):
    return jnp.pad(a, ((0, 0), (0, width - a.shape[1])))


def _regroup_w_in(w_in):
    sizes = (GLA_DK, GLA_DK, GLA_DV, GLA_DV, GLA_GATE_RANK, SSD_DINNER, SSD_CONV_DIM, SSD_HEADS)
    offs = [0]
    for s in sizes:
        offs.append(offs[-1] + s)
    q, k, v, r, glow, z, xbc, dt = (w_in[:, offs[i]:offs[i + 1]] for i in range(len(sizes)))
    return jnp.concatenate(
        [z, xbc[:, :SSD_DINNER], v, r, q, k, xbc[:, SSD_DINNER:], _pad_lanes(glow, LANES), _pad_lanes(dt, LANES)],
        axis=1).astype(BF16)


def _layer(h, mem, w_in, gla_wg2, gla_bg, gla_norm_g, conv_w, conv_b, dt_bias, a_log, d_skip, ssd_norm_g,
           w_out, ln1_g, ln1_b, ca_wq, ca_wk, ca_wv, ca_wo, ln2_g, ln2_b, peer_wq, peer_keys, peer_u,
           peer_v, ln3_g, ln3_b, bsz, seq):
    t = bsz * seq
    tm = min(512, seq)
    tc = min(512, seq)
    row = lambda a: a.reshape(1, -1)

    proj = _matmul(h, _regroup_w_in(w_in), tm, 512)
    wg_pad = jnp.pad(gla_wg2, ((0, LANES - GLA_GATE_RANK), (0, 0))).astype(BF16)
    o = _gla(proj, wg_pad, row(gla_bg), row(gla_norm_g), bsz, seq, tc)
    rep = lambda a: jnp.repeat(a.astype(F32), SSD_HEADDIM).reshape(1, SSD_DINNER)
    expand = (jnp.arange(LANES)[:, None] == (jnp.arange(SSD_DINNER)[None, :] // SSD_HEADDIM)).astype(F32)
    y = _ssd(proj, conv_w, row(conv_b), _pad_lanes(row(dt_bias), LANES), rep(-jnp.exp(a_log.astype(F32))),
             rep(d_skip), row(ssd_norm_g), expand, bsz, seq, tc)
    h1 = _outproj(o, y, h, w_out.astype(BF16), row(ln1_g), row(ln1_b), tm)

    mem_len = mem.shape[1]
    mem2 = mem.reshape(bsz * mem_len, D_MODEL)
    kv = _matmul(mem2, jnp.concatenate([ca_wk, ca_wv], axis=1).astype(BF16), min(512, bsz * mem_len), 512)
    h2 = _xattn(h1, kv[:, :D_MODEL], kv[:, D_MODEL:], ca_wq.astype(BF16), ca_wo.astype(BF16),
                row(ln2_g), row(ln2_b), bsz, seq, mem_len, tm)

    tt = min(256, t)
    i1, i2, gates, hb = _peer_select(h2, peer_wq.astype(BF16), peer_keys.astype(BF16), tt)
    eb = 2048
    val = _peer_act(hb, peer_u.astype(BF16), i1, i2, min(512, t), eb)
    return _peer_out(val, gates, i1, i2, peer_v.astype(BF16), h2, row(ln3_g), row(ln3_b), min(512, t), eb)


def kernel(x, mem, w_in, gla_wg2, gla_bg, gla_norm_g, ssd_conv_w, ssd_conv_b, ssd_dt_bias, ssd_a_log, ssd_d,
           ssd_norm_g, w_out, ln1_g, ln1_b, ca_wq, ca_wk, ca_wv, ca_wo, ln2_g, ln2_b, peer_wq, peer_keys,
           peer_u, peer_v, ln3_g, ln3_b):
    bsz, seq, _ = x.shape
    h = x.reshape(bsz * seq, D_MODEL)
    for l in range(w_in.shape[0]):
        h = _layer(h, mem, w_in[l], gla_wg2[l], gla_bg[l], gla_norm_g[l], ssd_conv_w[l], ssd_conv_b[l],
                   ssd_dt_bias[l], ssd_a_log[l], ssd_d[l], ssd_norm_g[l], w_out[l], ln1_g[l], ln1_b[l],
                   ca_wq[l], ca_wk[l], ca_wv[l], ca_wo[l], ln2_g[l], ln2_b[l], peer_wq[l], peer_keys[l],
                   peer_u[l], peer_v[l], ln3_g[l], ln3_b[l], bsz, seq)
    return h.reshape(bsz, seq, D_MODEL)
```

```python
import functools
import math

import jax
import jax.numpy as jnp
from jax import lax
from jax.experimental import pallas as pl
from jax.experimental.pallas import tpu as pltpu

F32 = jnp.float32
BF16 = jnp.bfloat16
HIGHEST = lax.Precision.HIGHEST

D_MODEL = 1024
GLA_HEADS = 4
GLA_HEAD_K = 128
GLA_HEAD_V = 256
GLA_DK = GLA_HEADS * GLA_HEAD_K
GLA_DV = GLA_HEADS * GLA_HEAD_V
GLA_GATE_RANK = 16
GLA_GATE_TAU = 16.0
CHUNK = 64
SSD_DINNER = 1024
SSD_HEADDIM = 64
SSD_HEADS = 16
SSD_GROUPS = 2
SSD_HPG = SSD_HEADS // SSD_GROUPS
SSD_STATE = 64
SSD_CONV = 4
SSD_CONV_DIM = SSD_DINNER + 2 * SSD_GROUPS * SSD_STATE
SSD_GROUP_W = SSD_DINNER // SSD_GROUPS
MEM_HEADS = 4
MEM_HEAD_DIM = 256
PEER_HEADS = 8
PEER_N_KEYS = 128
PEER_HALF = 64
PEER_TOPK = 16
DEPTH = 1
DEEPNORM_ALPHA = (2.0 * DEPTH) ** 0.25
LN_EPS = 1e-5
RMS_EPS = 1e-6

LANES = 128
SUBLANES = 8
VMEM_LIMIT = 56 * 1024 * 1024

GLA_COL_V = 0
GLA_COL_R = GLA_COL_V + GLA_DV
GLA_COL_Q = GLA_COL_R + GLA_DV
GLA_COL_K = GLA_COL_Q + GLA_DK
GLA_COL_GLOW = GLA_COL_K + GLA_DK
GLA_PROJ_W = GLA_COL_GLOW + LANES
SSD_COL_XBC = 0
SSD_COL_Z = SSD_COL_XBC + SSD_CONV_DIM
SSD_COL_DT = SSD_COL_Z + SSD_DINNER
SSD_PROJ_W = SSD_COL_DT + LANES
PROJ_N_TILE = 640


def _cparams(sem):
    return pltpu.CompilerParams(dimension_semantics=sem, vmem_limit_bytes=VMEM_LIMIT)


def _dot(a, b):
    return jnp.dot(a, b, preferred_element_type=F32)


def _dot_nt(a, b):
    return lax.dot_general(a, b, (((1,), (1,)), ((), ())), preferred_element_type=F32)


def _dot_tn(a, b):
    return lax.dot_general(a, b, (((0,), (0,)), ((), ())), preferred_element_type=F32)


def _dot_exact(a, b):
    return jnp.dot(a, b, preferred_element_type=F32, precision=HIGHEST)


def _silu(x):
    return x / (1.0 + jnp.exp(-x))


def _softplus(x):
    return jnp.maximum(x, 0.0) + jnp.log1p(jnp.exp(-jnp.abs(x)))


def _layer_norm(x, g, b):
    mu = jnp.mean(x, axis=-1, keepdims=True)
    xc = x - mu
    var = jnp.mean(xc * xc, axis=-1, keepdims=True)
    return xc * lax.rsqrt(var + LN_EPS) * g + b


def _matmul_kernel(x_ref, w_ref, o_ref):
    o_ref[...] = _dot(x_ref[...].astype(BF16), w_ref[...])


def _matmul(x, w, tm, tn):
    m, k = x.shape
    n = w.shape[1]
    return pl.pallas_call(
        _matmul_kernel,
        grid=(m // tm, n // tn),
        in_specs=[pl.BlockSpec((tm, k), lambda i, j: (i, 0)),
                  pl.BlockSpec((k, tn), lambda i, j: (0, j))],
        out_specs=pl.BlockSpec((tm, tn), lambda i, j: (i, j)),
        out_shape=jax.ShapeDtypeStruct((m, n), F32),
        compiler_params=_cparams(("arbitrary", "arbitrary")),
        name="matmul",
    )(x, w)


GLA_LEVELS = (32, 16, 8, 4, 2, 1)


def _gla_select_matrix():
    t = jnp.arange(CHUNK)
    tril = (t[:, None] >= t[None, :]).astype(F32)
    mats = [tril]
    for c in GLA_LEVELS:
        mid = (t // (2 * c)) * (2 * c) + c
        mats.append(tril[mid])
    return jnp.concatenate(mats, axis=0)


def _split3(x):
    hi = x.astype(BF16)
    r1 = x - hi.astype(F32)
    mid = r1.astype(BF16)
    lo = (r1 - mid.astype(F32)).astype(BF16)
    return hi, mid, lo


def _dot_sel_exact(sel, x):
    hi, mid, lo = _split3(x)
    return _dot(sel, hi) + _dot(sel, mid) + _dot(sel, lo)


def _dot_exact_sel(x, sel):
    hi, mid, lo = _split3(x)
    return _dot(hi, sel) + _dot(mid, sel) + _dot(lo, sel)


def _project(xb, w_ref, dst_ref, row0, col0, width):
    rows = xb.shape[0]
    for n in range(0, width, PROJ_N_TILE):
        wn = min(PROJ_N_TILE, width - n)
        dst_ref[row0:row0 + rows, n:n + wn] = _dot(xb, w_ref[:, col0 + n:col0 + n + wn])


def _gla_kernel(x_ref, w_ref, wg_ref, bg_ref, gn_ref, sel_ref, o_ref,
                state_ref, loga_ref, proj_ref, *, tc):
    @pl.when(pl.program_id(1) == 0)
    def _():
        state_ref[...] = jnp.zeros_like(state_ref)

    _project(x_ref[...].astype(BF16), w_ref, proj_ref, 0, 0, GLA_PROJ_W)
    gate_pre = _dot(proj_ref[:, GLA_COL_GLOW:GLA_COL_GLOW + LANES].astype(BF16), wg_ref[...]) + bg_ref[...]
    loga_ref[...] = -_softplus(-gate_pre) * (1.0 / GLA_GATE_TAU)

    row = lax.broadcasted_iota(jnp.int32, (CHUNK, CHUNK), 0)
    col = lax.broadcasted_iota(jnp.int32, (CHUNK, CHUNK), 1)
    masks = []
    for cs in GLA_LEVELS:
        same_blk = (row // (2 * cs)) == (col // (2 * cs))
        masks.append(same_blk & ((row % (2 * cs)) >= cs) & ((col % (2 * cs)) < cs))
    diag = row == col
    scale = GLA_HEAD_K ** -0.5

    def chunk(c, carry):
        r0 = pl.multiple_of(c * CHUNK, CHUNK)
        rows = pl.ds(r0, CHUNK)
        ball_all = _dot_sel_exact(sel_ref[...], loga_ref[rows, :])
        for h in range(GLA_HEADS):
            ks = slice(h * GLA_HEAD_K, (h + 1) * GLA_HEAD_K)
            vs = slice(h * GLA_HEAD_V, (h + 1) * GLA_HEAD_V)
            ball = ball_all[:, ks]
            b = ball[0:CHUNK]
            q = proj_ref[rows, pl.ds(GLA_COL_Q + h * GLA_HEAD_K, GLA_HEAD_K)] * scale
            k = proj_ref[rows, pl.ds(GLA_COL_K + h * GLA_HEAD_K, GLA_HEAD_K)]
            v = proj_ref[rows, pl.ds(GLA_COL_V + h * GLA_HEAD_V, GLA_HEAD_V)].astype(BF16)
            scores = jnp.where(diag, _dot_nt(q.astype(BF16), k.astype(BF16)), 0.0)
            for li in range(len(GLA_LEVELS)):
                d = b - ball[(li + 1) * CHUNK:(li + 2) * CHUNK]
                qe = q * jnp.exp(jnp.minimum(d, 0.0))
                ke = k * jnp.exp(jnp.minimum(-d, 0.0))
                scores = jnp.where(masks[li], _dot_nt(qe.astype(BF16), ke.astype(BF16)), scores)
            state = state_ref[h]
            o = _dot(scores.astype(BF16), v) + _dot((q * jnp.exp(b)).astype(BF16), state.astype(BF16))
            b_last = b[CHUNK - 1:CHUNK, :]
            kd = k * jnp.exp(b_last - b)
            decay = jnp.exp(jnp.broadcast_to(b_last, (GLA_HEAD_K, GLA_HEAD_K)).T)
            state_ref[h] = (jnp.concatenate([decay] * (GLA_HEAD_V // GLA_HEAD_K), axis=1) * state
                            + _dot_tn(kd.astype(BF16), v))
            o = o * lax.rsqrt(jnp.mean(o * o, axis=-1, keepdims=True) + RMS_EPS)
            r = proj_ref[rows, pl.ds(GLA_COL_R + h * GLA_HEAD_V, GLA_HEAD_V)]
            o_ref[rows, vs] = o * gn_ref[...] * _silu(r)
        return carry

    lax.fori_loop(0, tc // CHUNK, chunk, 0)


def _gla(x, w, wg_pad, bg, gn, bsz, seq, tc):
    nt = seq // tc
    t = bsz * seq
    kern = functools.partial(_gla_kernel, tc=tc)
    full = lambda shape: pl.BlockSpec(shape, lambda b, i: tuple(0 for _ in shape))
    return pl.pallas_call(
        kern,
        grid=(bsz, nt),
        in_specs=[
            pl.BlockSpec((tc, D_MODEL), lambda b, i: (b * nt + i, 0)),
            full((D_MODEL, GLA_PROJ_W)),
            full((LANES, GLA_DK)), full((1, GLA_DK)), full((1, GLA_HEAD_V)),
            full(((len(GLA_LEVELS) + 1) * CHUNK, CHUNK)),
        ],
        out_specs=pl.BlockSpec((tc, GLA_DV), lambda b, i: (b * nt + i, 0)),
        out_shape=jax.ShapeDtypeStruct((t, GLA_DV), F32),
        scratch_shapes=[pltpu.VMEM((GLA_HEADS, GLA_HEAD_K, GLA_HEAD_V), F32),
                        pltpu.VMEM((tc, GLA_DK), F32),
                        pltpu.VMEM((tc, GLA_PROJ_W), F32)],
        compiler_params=_cparams(("arbitrary", "arbitrary")),
        name="gla",
    )(x, w, wg_pad, bg, gn, _gla_select_matrix().astype(BF16))


CONV_PAD = SUBLANES


def _ssd_kernel(x_ref, w_ref, cw_ref, cb_ref, dtb_ref, aneg_ref, anegs_ref, dsk_ref, ng_ref, exp_ref,
                o_ref, xs_ref, z_ref, dt_ref, h_ref, yd_ref, *, tc):
    @pl.when(pl.program_id(1) == 0)
    def _():
        xs_ref[0:CONV_PAD, :] = jnp.zeros((CONV_PAD, SSD_CONV_DIM), F32)
        h_ref[...] = jnp.zeros_like(h_ref)

    xb = x_ref[...].astype(BF16)
    _project(xb, w_ref, xs_ref, CONV_PAD, SSD_COL_XBC, SSD_CONV_DIM)
    _project(xb, w_ref, z_ref, 0, SSD_COL_Z, SSD_DINNER)
    _project(xb, w_ref, dt_ref, 0, SSD_COL_DT, LANES)

    row = lax.broadcasted_iota(jnp.int32, (CHUNK, CHUNK), 0)
    col = lax.broadcasted_iota(jnp.int32, (CHUNK, CHUNK), 1)
    causal = row >= col
    tril = jnp.where(causal, 1.0, 0.0).astype(BF16)
    n0 = SSD_DINNER
    n1 = SSD_DINNER + SSD_GROUPS * SSD_STATE

    def chunk(c):
        r0 = c * CHUNK
        rows = pl.ds(r0, CHUNK)
        conv = cb_ref[...]
        for kk in range(SSD_CONV):
            off = CONV_PAD - (SSD_CONV - 1) + kk
            conv = conv + cw_ref[kk:kk + 1, :] * xs_ref[pl.ds(r0 + off, CHUNK), :]
        xbc = _silu(conv)
        x = xbc[:, 0:n0]
        dt = _softplus(dt_ref[rows, :] + dtb_ref[...])
        dtc = _dot_exact_sel(dt, exp_ref[...])
        cs = _dot_sel_exact(tril, dtc * aneg_ref[...])
        cs_t = _dot_sel_exact(tril, dt * anegs_ref[...]).T
        xdt = x * dtc
        cs_last = cs[CHUNK - 1:CHUNK, :]
        xw = (xdt * jnp.exp(cs_last - cs)).astype(BF16)
        xdt_b = xdt.astype(BF16)
        ecs = jnp.exp(cs)
        ecl = jnp.exp(cs_last)
        y_parts = []
        for g in range(SSD_GROUPS):
            lo = g * SSD_GROUP_W
            bg = xbc[:, n0 + g * SSD_STATE:n0 + (g + 1) * SSD_STATE].astype(BF16)
            cg = xbc[:, n1 + g * SSD_STATE:n1 + (g + 1) * SSD_STATE].astype(BF16)
            cb = _dot_nt(cg, bg)
            hg = h_ref[g]
            y_off = _dot(cg, hg.astype(BF16)) * ecs[:, lo:lo + SSD_GROUP_W]
            for hh in range(SSD_HPG):
                hd = g * SSD_HPG + hh
                c0 = lo + hh * SSD_HEADDIM
                colv = cs[:, c0:c0 + SSD_HEADDIM]
                rowv = jnp.broadcast_to(cs_t[hd:hd + 1, :], (CHUNK, CHUNK))
                lm = jnp.exp(jnp.where(causal, colv - rowv, -jnp.inf))
                yd_ref[:, c0:c0 + SSD_HEADDIM] = _dot((cb * lm).astype(BF16), xdt_b[:, c0:c0 + SSD_HEADDIM])
            states = _dot_tn(bg, xw[:, lo:lo + SSD_GROUP_W])
            h_ref[g] = hg * ecl[:, lo:lo + SSD_GROUP_W] + states
            y_parts.append(y_off)
        y = yd_ref[...] + jnp.concatenate(y_parts, axis=1) + dsk_ref[...] * x
        y = y * _silu(z_ref[rows, :])
        outs = []
        for g in range(SSD_GROUPS):
            yg = y[:, g * SSD_GROUP_W:(g + 1) * SSD_GROUP_W]
            outs.append(yg * lax.rsqrt(jnp.mean(yg * yg, axis=-1, keepdims=True) + RMS_EPS))
        o_ref[rows, :] = jnp.concatenate(outs, axis=1) * ng_ref[...]

    for c in range(tc // CHUNK):
        chunk(c)
    xs_ref[0:CONV_PAD, :] = xs_ref[tc:tc + CONV_PAD, :]


def _ssd(x, w, cw, cb, dtb, aneg, anegs, dsk, ng, expand, bsz, seq, tc):
    nt = seq // tc
    t = bsz * seq
    kern = functools.partial(_ssd_kernel, tc=tc)
    full = lambda shape: pl.BlockSpec(shape, lambda b, i: tuple(0 for _ in shape))
    return pl.pallas_call(
        kern,
        grid=(bsz, nt),
        in_specs=[
            pl.BlockSpec((tc, D_MODEL), lambda b, i: (b * nt + i, 0)),
            full((D_MODEL, SSD_PROJ_W)),
            full((SSD_CONV, SSD_CONV_DIM)), full((1, SSD_CONV_DIM)), full((1, LANES)),
            full((1, SSD_DINNER)), full((1, LANES)), full((1, SSD_DINNER)), full((1, SSD_DINNER)),
            full((LANES, SSD_DINNER)),
        ],
        out_specs=pl.BlockSpec((tc, SSD_DINNER), lambda b, i: (b * nt + i, 0)),
        out_shape=jax.ShapeDtypeStruct((t, SSD_DINNER), F32),
        scratch_shapes=[pltpu.VMEM((tc + CONV_PAD, SSD_CONV_DIM), F32),
                        pltpu.VMEM((tc, SSD_DINNER), F32),
                        pltpu.VMEM((tc, LANES), F32),
                        pltpu.VMEM((SSD_GROUPS, SSD_STATE, SSD_GROUP_W), F32),
                        pltpu.VMEM((CHUNK, SSD_DINNER), F32)],
        compiler_params=_cparams(("arbitrary", "arbitrary")),
        name="ssd",
    )(x, w, cw, cb, dtb, aneg, anegs, dsk, ng, expand)


def _outproj_kernel(o_ref, y_ref, x_ref, w_ref, g_ref, b_ref, h_ref):
    mix = (_dot(o_ref[...].astype(BF16), w_ref[0:GLA_DV, :])
           + _dot(y_ref[...].astype(BF16), w_ref[GLA_DV:GLA_DV + SSD_DINNER, :]))
    h_ref[...] = _layer_norm(DEEPNORM_ALPHA * x_ref[...] + mix, g_ref[...], b_ref[...])


def _outproj(o, y, x, w, g, b, tm):
    t = x.shape[0]
    tile = lambda w_: pl.BlockSpec((tm, w_), lambda i: (i, 0))
    full = lambda shape: pl.BlockSpec(shape, lambda i: tuple(0 for _ in shape))
    return pl.pallas_call(
        _outproj_kernel,
        grid=(t // tm,),
        in_specs=[tile(GLA_DV), tile(SSD_DINNER), tile(D_MODEL), full(w.shape),
                  full((1, D_MODEL)), full((1, D_MODEL))],
        out_specs=tile(D_MODEL),
        out_shape=jax.ShapeDtypeStruct((t, D_MODEL), F32),
        compiler_params=_cparams(("arbitrary",)),
        name="outproj_ln",
    )(o, y, x, w, g, b)


def _xattn_kernel(h_ref, k_ref, v_ref, wq_ref, wo_ref, g_ref, b_ref, o_ref):
    h = h_ref[...]
    q = _dot(h.astype(BF16), wq_ref[...]).astype(BF16)
    scale = MEM_HEAD_DIM ** -0.5
    heads = []
    for hd in range(MEM_HEADS):
        sl = slice(hd * MEM_HEAD_DIM, (hd + 1) * MEM_HEAD_DIM)
        s = _dot_nt(q[:, sl], k_ref[:, sl].astype(BF16)) * scale
        s = s - jnp.max(s, axis=-1, keepdims=True)
        e = jnp.exp(s)
        p = e / jnp.sum(e, axis=-1, keepdims=True)
        heads.append(_dot(p.astype(BF16), v_ref[:, sl].astype(BF16)))
    ca = _dot(jnp.concatenate(heads, axis=1).astype(BF16), wo_ref[...])
    o_ref[...] = _layer_norm(DEEPNORM_ALPHA * h + ca, g_ref[...], b_ref[...])


def _xattn(h, kv, wq, wo, g, b, bsz, seq, mem_len, tm):
    nt = seq // tm
    t = bsz * seq
    full = lambda shape: pl.BlockSpec(shape, lambda bb, i: tuple(0 for _ in shape))
    return pl.pallas_call(
        _xattn_kernel,
        grid=(bsz, nt),
        in_specs=[pl.BlockSpec((tm, D_MODEL), lambda bb, i: (bb * nt + i, 0)),
                  pl.BlockSpec((mem_len, D_MODEL), lambda bb, i: (bb, 0)),
                  pl.BlockSpec((mem_len, D_MODEL), lambda bb, i: (bb, 1)),
                  full((D_MODEL, D_MODEL)), full((D_MODEL, D_MODEL)),
                  full((1, D_MODEL)), full((1, D_MODEL))],
        out_specs=pl.BlockSpec((tm, D_MODEL), lambda bb, i: (bb * nt + i, 0)),
        out_shape=jax.ShapeDtypeStruct((t, D_MODEL), F32),
        compiler_params=_cparams(("arbitrary", "arbitrary")),
        name="xattn_ln",
    )(h, kv, kv, wq, wo, g, b)


N_SEL = PEER_HEADS * PEER_TOPK
TOK_GROUP = 32
NEG_INF = float("-inf")

_CAND_GROUPS = ((0, 0, 8), (0, 8, 8), (1, 0, 8), (2, 0, 5), (3, 0, 4), (4, 0, 3), (5, 0, 2), (6, 0, 2), (7, 0, 2))


def _peer_select_kernel(h_ref, wq_ref, keys_ref, i1_ref, i2_ref, g_ref, hb_ref,
                        st_ref, it_ref, i1s_ref, i2s_ref, gs_ref, *, tt):
    hb = h_ref[...].astype(BF16)
    hb_ref[...] = hb
    q = _dot(hb, wq_ref[...]).astype(BF16)
    key_iota = lax.broadcasted_iota(jnp.int32, (PEER_N_KEYS, tt), 0).astype(F32)
    row8 = lax.broadcasted_iota(jnp.int32, (SUBLANES, tt), 0).astype(F32)
    row16 = lax.broadcasted_iota(jnp.int32, (PEER_TOPK, tt), 0).astype(F32)
    big = float(PEER_TOPK * PEER_TOPK)

    for head in range(PEER_HEADS):
        for p in range(2):
            c0 = (head * 2 + p) * PEER_HALF
            x = _dot_nt(keys_ref[p], q[:, c0:c0 + PEER_HALF])
            for kk in range(PEER_TOPK):
                m = jnp.max(x, axis=0, keepdims=True)
                idx = jnp.min(jnp.where(x == m, key_iota, float(PEER_N_KEYS)), axis=0, keepdims=True)
                x = jnp.where(key_iota == idx, NEG_INF, x)
                st_ref[p, kk:kk + 1, :] = m
                it_ref[p, kk:kk + 1, :] = idx
        s1 = st_ref[0]
        s2 = st_ref[1]
        vals, flats = [], []
        for (a, b0, cnt) in _CAND_GROUPS:
            v = s1[a:a + 1, :] + s2[b0:b0 + SUBLANES, :]
            vals.append(jnp.where(row8 < float(cnt), v, NEG_INF))
            flats.append(float(a * PEER_TOPK + b0) + row8)
        vals.append(s1[SUBLANES:PEER_TOPK, :] + s2[0:1, :])
        flats.append((float(SUBLANES) + row8) * float(PEER_TOPK))
        best_rows = []
        for kk in range(PEER_TOPK):
            m = functools.reduce(jnp.maximum, vals)
            m = jnp.max(m, axis=0, keepdims=True)
            cand = functools.reduce(jnp.minimum,
                                    [jnp.where(v == m, f, big) for v, f in zip(vals, flats)])
            cidx = jnp.min(cand, axis=0, keepdims=True)
            vals = [jnp.where(f == cidx, NEG_INF, v) for v, f in zip(vals, flats)]
            a_sel = jnp.floor(cidx * (1.0 / PEER_TOPK))
            b_sel = cidx - a_sel * float(PEER_TOPK)
            i1 = jnp.sum(jnp.where(row16 == a_sel, it_ref[0], 0.0), axis=0, keepdims=True)
            i2 = jnp.sum(jnp.where(row16 == b_sel, it_ref[1], 0.0), axis=0, keepdims=True)
            r = head * PEER_TOPK + kk
            i1s_ref[r:r + 1, :] = i1
            i2s_ref[r:r + 1, :] = i2
            best_rows.append(m)
        e = [jnp.exp(b - best_rows[0]) for b in best_rows]
        denom = functools.reduce(lambda u, w: u + w, e)
        for kk in range(PEER_TOPK):
            r = head * PEER_TOPK + kk
            gs_ref[r:r + 1, :] = e[kk] / denom

    i1_ref[...] = i1s_ref[...].T.astype(jnp.int32)
    i2_ref[...] = i2s_ref[...].T.astype(jnp.int32)
    g_ref[...] = gs_ref[...].T


def _peer_select(h, wq, keys, tt):
    t = h.shape[0]
    kern = functools.partial(_peer_select_kernel, tt=tt)
    tile = lambda w_: pl.BlockSpec((tt, w_), lambda i: (i, 0))
    full = lambda shape: pl.BlockSpec(shape, lambda i: tuple(0 for _ in shape))
    return pl.pallas_call(
        kern,
        grid=(t // tt,),
        in_specs=[tile(D_MODEL), full((D_MODEL, D_MODEL)), full((2, PEER_N_KEYS, PEER_HALF))],
        out_specs=[tile(N_SEL), tile(N_SEL), tile(N_SEL), tile(D_MODEL)],
        out_shape=[jax.ShapeDtypeStruct((t, N_SEL), jnp.int32),
                   jax.ShapeDtypeStruct((t, N_SEL), jnp.int32),
                   jax.ShapeDtypeStruct((t, N_SEL), F32),
                   jax.ShapeDtypeStruct((t, D_MODEL), BF16)],
        scratch_shapes=[pltpu.VMEM((2, PEER_TOPK, tt), F32),
                        pltpu.VMEM((2, PEER_TOPK, tt), F32),
                        pltpu.VMEM((N_SEL, tt), F32),
                        pltpu.VMEM((N_SEL, tt), F32),
                        pltpu.VMEM((N_SEL, tt), F32)],
        compiler_params=_cparams(("arbitrary",)),
        name="peer_select",
    )(h, wq, keys)


def _peer_act_kernel(hb_ref, u_ref, i1_ref, i2_ref, val_ref, *, eb):
    act = _dot_nt(hb_ref[...], u_ref[...])
    i1 = i1_ref[...]
    i2 = i2_ref[...]
    slabs = eb // PEER_N_KEYS
    slab0 = pl.program_id(0) * slabs
    acc = jnp.zeros(i1.shape, F32)
    for s in range(slabs):
        picked = jnp.take_along_axis(act[:, s * PEER_N_KEYS:(s + 1) * PEER_N_KEYS], i2, axis=1)
        acc = jnp.where(i1 == slab0 + s, picked, acc)
    val_ref[0] = acc


def _peer_act(hb, u, i1, i2, tt, eb):
    t = hb.shape[0]
    ne = u.shape[0] // eb
    kern = functools.partial(_peer_act_kernel, eb=eb)
    return pl.pallas_call(
        kern,
        grid=(ne, t // tt),
        in_specs=[pl.BlockSpec((tt, D_MODEL), lambda e, i: (i, 0)),
                  pl.BlockSpec((eb, D_MODEL), lambda e, i: (e, 0)),
                  pl.BlockSpec((tt, N_SEL), lambda e, i: (i, 0)),
                  pl.BlockSpec((tt, N_SEL), lambda e, i: (i, 0))],
        out_specs=pl.BlockSpec((1, tt, N_SEL), lambda e, i: (e, i, 0)),
        out_shape=jax.ShapeDtypeStruct((ne, t, N_SEL), F32),
        compiler_params=_cparams(("arbitrary", "arbitrary")),
        name="peer_act",
    )(hb, u, i1, i2)


def _slab_pitch(tt):
    octets = tt // SUBLANES
    return SUBLANES * (octets + 1 if octets % 2 == 0 else octets)


def _peer_out_kernel(val_ref, g_ref, i1_ref, i2_ref, v_ref, h_ref, lg_ref, lb_ref, o_ref,
                     w_ref, wslab_ref, acc_ref, *, tt, eb, ne):
    j = pl.program_id(1)
    pitch = _slab_pitch(tt)

    @pl.when(j == 0)
    def _():
        val = val_ref[0]
        for e in range(1, ne):
            val = val + val_ref[e]
        gelu = 0.5 * val * (1.0 + lax.erf(val * (2.0 ** -0.5)))
        w_ref[...] = g_ref[...] * gelu
        acc_ref[...] = jnp.zeros_like(acc_ref)
        sub = lax.broadcasted_iota(jnp.int32, (PEER_N_KEYS, N_SEL), 0).astype(F32).astype(BF16)
        one = jnp.ones((PEER_N_KEYS, N_SEL), BF16)
        zero = jnp.zeros((PEER_N_KEYS, N_SEL), BF16)

        def tok_group(gi, carry):
            t0 = pl.multiple_of(gi * TOK_GROUP, TOK_GROUP)
            i1g = i1_ref[pl.ds(t0, TOK_GROUP), :].astype(F32).astype(BF16)
            i2g = i2_ref[pl.ds(t0, TOK_GROUP), :].astype(F32).astype(BF16)
            wg = w_ref[pl.ds(t0, TOK_GROUP), :].astype(BF16)
            for u in range(TOK_GROUP):
                p1 = jnp.where(sub == i1g[u:u + 1, :], one, zero)
                p2w = jnp.where(sub == i2g[u:u + 1, :], wg[u:u + 1, :], zero)
                g = _dot_nt(p1, p2w)
                for o in range(PEER_N_KEYS // SUBLANES):
                    wslab_ref[pl.ds(o * SUBLANES * pitch + t0 + u, SUBLANES, stride=pitch), :] = (
                        g[o * SUBLANES:(o + 1) * SUBLANES, :])
            return carry

        lax.fori_loop(0, tt // TOK_GROUP, tok_group, 0)

    slabs = eb // PEER_N_KEYS
    cols = []
    for s in range(slabs):
        r0 = pl.multiple_of((j * slabs + s) * pitch, SUBLANES)
        cols.append(wslab_ref[pl.ds(r0, tt), :].astype(BF16))
    acc_ref[...] += _dot(jnp.concatenate(cols, axis=1), v_ref[...])

    @pl.when(j == ne - 1)
    def _():
        o_ref[...] = _layer_norm(DEEPNORM_ALPHA * h_ref[...] + acc_ref[...], lg_ref[...], lb_ref[...])


def _peer_out(val, gates, i1, i2, v, h, lg, lb, tt, eb):
    t = h.shape[0]
    ne = v.shape[0] // eb
    kern = functools.partial(_peer_out_kernel, tt=tt, eb=eb, ne=ne)
    tile = lambda w_: pl.BlockSpec((tt, w_), lambda i, j: (i, 0))
    full = lambda shape: pl.BlockSpec(shape, lambda i, j: tuple(0 for _ in shape))
    return pl.pallas_call(
        kern,
        grid=(t // tt, ne),
        in_specs=[pl.BlockSpec((ne, tt, N_SEL), lambda i, j: (0, i, 0)),
                  tile(N_SEL), tile(N_SEL), tile(N_SEL),
                  pl.BlockSpec((eb, D_MODEL), lambda i, j: (j, 0)),
                  tile(D_MODEL), full((1, D_MODEL)), full((1, D_MODEL))],
        out_specs=tile(D_MODEL),
        out_shape=jax.ShapeDtypeStruct((t, D_MODEL), F32),
        scratch_shapes=[pltpu.VMEM((tt, N_SEL), F32),
                        pltpu.VMEM((PEER_N_KEYS * _slab_pitch(tt), PEER_N_KEYS), F32),
                        pltpu.VMEM((tt, D_MODEL), F32)],
        compiler_params=_cparams(("arbitrary", "arbitrary")),
        name="peer_out",
    )(val, gates, i1, i2, v, h, lg, lb)


def _pad_lanes(a, width):
    return jnp.pad(a, ((0, 0), (0, width - a.shape[1])))


def _regroup_w_in(w_in):
    sizes = (GLA_DK, GLA_DK, GLA_DV, GLA_DV, GLA_GATE_RANK, SSD_DINNER, SSD_CONV_DIM, SSD_HEADS)
    offs = [0]
    for s in sizes:
        offs.append(offs[-1] + s)
    q, k, v, r, glow, z, xbc, dt = (w_in[:, offs[i]:offs[i + 1]] for i in range(len(sizes)))
    w_gla = jnp.concatenate([v, r, q, k, _pad_lanes(glow, LANES)], axis=1).astype(BF16)
    w_ssd = jnp.concatenate([xbc, z, _pad_lanes(dt, LANES)], axis=1).astype(BF16)
    return w_gla, w_ssd


def _layer(h, mem, w_in, gla_wg2, gla_bg, gla_norm_g, conv_w, conv_b, dt_bias, a_log, d_skip, ssd_norm_g,
           w_out, ln1_g, ln1_b, ca_wq, ca_wk, ca_wv, ca_wo, ln2_g, ln2_b, peer_wq, peer_keys, peer_u,
           peer_v, ln3_g, ln3_b, bsz, seq):
    t = bsz * seq
    tm = min(512, seq)
    tc = min(512, seq)
    row = lambda a: a.reshape(1, -1)

    w_gla, w_ssd = _regroup_w_in(w_in)
    wg_pad = jnp.pad(gla_wg2, ((0, LANES - GLA_GATE_RANK), (0, 0))).astype(BF16)
    o = _gla(h, w_gla, wg_pad, row(gla_bg), row(gla_norm_g), bsz, seq, tc)
    rep = lambda a: jnp.repeat(a.astype(F32), SSD_HEADDIM).reshape(1, SSD_DINNER)
    expand = (jnp.arange(LANES)[:, None] == (jnp.arange(SSD_DINNER)[None, :] // SSD_HEADDIM)).astype(BF16)
    a_neg = -jnp.exp(a_log.astype(F32))
    y = _ssd(h, w_ssd, conv_w, row(conv_b), _pad_lanes(row(dt_bias), LANES), rep(a_neg),
             _pad_lanes(row(a_neg), LANES), rep(d_skip), row(ssd_norm_g), expand, bsz, seq, tc)
    h1 = _outproj(o, y, h, w_out.astype(BF16), row(ln1_g), row(ln1_b), tm)

    mem_len = mem.shape[1]
    mem2 = mem.reshape(bsz * mem_len, D_MODEL)
    kv = _matmul(mem2, jnp.concatenate([ca_wk, ca_wv], axis=1).astype(BF16), min(512, bsz * mem_len), 512)
    h2 = _xattn(h1, kv, ca_wq.astype(BF16), ca_wo.astype(BF16),
                row(ln2_g), row(ln2_b), bsz, seq, mem_len, tm)

    tt = min(256, t)
    i1, i2, gates, hb = _peer_select(h2, peer_wq.astype(BF16), peer_keys.astype(BF16), tt)
    eb = 2048
    val = _peer_act(hb, peer_u.astype(BF16), i1, i2, min(512, t), eb)
    return _peer_out(val, gates, i1, i2, peer_v.astype(BF16), h2, row(ln3_g), row(ln3_b), min(256, t), eb)


def kernel(x, mem, w_in, gla_wg2, gla_bg, gla_norm_g, ssd_conv_w, ssd_conv_b, ssd_dt_bias, ssd_a_log, ssd_d,
           ssd_norm_g, w_out, ln1_g, ln1_b, ca_wq, ca_wk, ca_wv, ca_wo, ln2_g, ln2_b, peer_wq, peer_keys,
           peer_u, peer_v, ln3_g, ln3_b):
    bsz, seq, _ = x.shape
    h = x.reshape(bsz * seq, D_MODEL)
    for l in range(w_in.shape[0]):
        h = _layer(h, mem, w_in[l], gla_wg2[l], gla_bg[l], gla_norm_g[l], ssd_conv_w[l], ssd_conv_b[l],
                   ssd_dt_bias[l], ssd_a_log[l], ssd_d[l], ssd_norm_g[l], w_out[l], ln1_g[l], ln1_b[l],
                   ca_wq[l], ca_wk[l], ca_wv[l], ca_wo[l], ln2_g[l], ln2_b[l], peer_wq[l], peer_keys[l],
                   peer_u[l], peer_v[l], ln3_g[l], ln3_b[l], bsz, seq)
    return h.reshape(bsz, seq, D_MODEL)
```

```python
import functools
import math

import jax
import jax.numpy as jnp
from jax import lax
from jax.experimental import pallas as pl
from jax.experimental.pallas import tpu as pltpu

F32 = jnp.float32
BF16 = jnp.bfloat16
HIGHEST = lax.Precision.HIGHEST

D_MODEL = 1024
GLA_HEADS = 4
GLA_HEAD_K = 128
GLA_HEAD_V = 256
GLA_DK = GLA_HEADS * GLA_HEAD_K
GLA_DV = GLA_HEADS * GLA_HEAD_V
GLA_GATE_RANK = 16
GLA_GATE_TAU = 16.0
CHUNK = 64
SSD_DINNER = 1024
SSD_HEADDIM = 64
SSD_HEADS = 16
SSD_GROUPS = 2
SSD_HPG = SSD_HEADS // SSD_GROUPS
SSD_STATE = 64
SSD_CONV = 4
SSD_CONV_DIM = SSD_DINNER + 2 * SSD_GROUPS * SSD_STATE
SSD_GROUP_W = SSD_DINNER // SSD_GROUPS
MEM_HEADS = 4
MEM_HEAD_DIM = 256
PEER_HEADS = 8
PEER_N_KEYS = 128
PEER_HALF = 64
PEER_TOPK = 16
DEPTH = 1
DEEPNORM_ALPHA = (2.0 * DEPTH) ** 0.25
LN_EPS = 1e-5
RMS_EPS = 1e-6

LANES = 128
SUBLANES = 8
VMEM_LIMIT = 56 * 1024 * 1024

GLA_COL_V = 0
GLA_COL_R = GLA_COL_V + GLA_DV
GLA_COL_Q = GLA_COL_R + GLA_DV
GLA_COL_K = GLA_COL_Q + GLA_DK
GLA_COL_GLOW = GLA_COL_K + GLA_DK
GLA_PROJ_W = GLA_COL_GLOW + LANES
SSD_COL_XBC = 0
SSD_COL_Z = SSD_COL_XBC + SSD_CONV_DIM
SSD_COL_DT = SSD_COL_Z + SSD_DINNER
SSD_PROJ_W = SSD_COL_DT + LANES
PROJ_N_TILE = 640


def _cparams(sem):
    return pltpu.CompilerParams(dimension_semantics=sem, vmem_limit_bytes=VMEM_LIMIT)


def _dot(a, b):
    return jnp.dot(a, b, preferred_element_type=F32)


def _dot_nt(a, b):
    return lax.dot_general(a, b, (((1,), (1,)), ((), ())), preferred_element_type=F32)


def _dot_tn(a, b):
    return lax.dot_general(a, b, (((0,), (0,)), ((), ())), preferred_element_type=F32)


def _dot_exact(a, b):
    return jnp.dot(a, b, preferred_element_type=F32, precision=HIGHEST)


def _silu(x):
    return x / (1.0 + jnp.exp(-x))


def _softplus(x):
    return jnp.maximum(x, 0.0) + jnp.log1p(jnp.exp(-jnp.abs(x)))


def _layer_norm(x, g, b):
    mu = jnp.mean(x, axis=-1, keepdims=True)
    xc = x - mu
    var = jnp.mean(xc * xc, axis=-1, keepdims=True)
    return xc * lax.rsqrt(var + LN_EPS) * g + b


def _matmul_kernel(x_ref, w_ref, o_ref):
    o_ref[...] = _dot(x_ref[...].astype(BF16), w_ref[...])


def _matmul(x, w, tm, tn):
    m, k = x.shape
    n = w.shape[1]
    return pl.pallas_call(
        _matmul_kernel,
        grid=(m // tm, n // tn),
        in_specs=[pl.BlockSpec((tm, k), lambda i, j: (i, 0)),
                  pl.BlockSpec((k, tn), lambda i, j: (0, j))],
        out_specs=pl.BlockSpec((tm, tn), lambda i, j: (i, j)),
        out_shape=jax.ShapeDtypeStruct((m, n), F32),
        compiler_params=_cparams(("arbitrary", "arbitrary")),
        name="matmul",
    )(x, w)


GLA_LEVELS = (32, 16, 8, 4, 2, 1)


def _gla_select_matrix():
    t = jnp.arange(CHUNK)
    tril = (t[:, None] >= t[None, :]).astype(F32)
    mats = [tril]
    for c in GLA_LEVELS:
        mid = (t // (2 * c)) * (2 * c) + c
        mats.append(tril[mid])
    return jnp.concatenate(mats, axis=0)


def _split3(x):
    hi = x.astype(BF16)
    r1 = x - hi.astype(F32)
    mid = r1.astype(BF16)
    lo = (r1 - mid.astype(F32)).astype(BF16)
    return hi, mid, lo


def _dot_sel_exact(sel, x):
    hi, mid, lo = _split3(x)
    return _dot(sel, hi) + _dot(sel, mid) + _dot(sel, lo)


def _dot_exact_sel(x, sel):
    hi, mid, lo = _split3(x)
    return _dot(hi, sel) + _dot(mid, sel) + _dot(lo, sel)


def _project(xb, w_ref, dst_ref, row0, col0, width):
    rows = xb.shape[0]
    for n in range(0, width, PROJ_N_TILE):
        wn = min(PROJ_N_TILE, width - n)
        dst_ref[row0:row0 + rows, n:n + wn] = _dot(xb, w_ref[:, col0 + n:col0 + n + wn])


def _gla_kernel(x_ref, w_ref, wg_ref, bg_ref, gn_ref, sel_ref, o_ref,
                state_ref, loga_ref, proj_ref, *, tc):
    @pl.when(pl.program_id(1) == 0)
    def _():
        state_ref[...] = jnp.zeros_like(state_ref)

    _project(x_ref[...].astype(BF16), w_ref, proj_ref, 0, 0, GLA_PROJ_W)
    gate_pre = _dot(proj_ref[:, GLA_COL_GLOW:GLA_COL_GLOW + LANES].astype(BF16), wg_ref[...]) + bg_ref[...]
    loga_ref[...] = -_softplus(-gate_pre) * (1.0 / GLA_GATE_TAU)

    row = lax.broadcasted_iota(jnp.int32, (CHUNK, CHUNK), 0)
    col = lax.broadcasted_iota(jnp.int32, (CHUNK, CHUNK), 1)
    masks = []
    for cs in GLA_LEVELS:
        same_blk = (row // (2 * cs)) == (col // (2 * cs))
        masks.append(same_blk & ((row % (2 * cs)) >= cs) & ((col % (2 * cs)) < cs))
    diag = row == col
    scale = GLA_HEAD_K ** -0.5

    def chunk(c, carry):
        r0 = pl.multiple_of(c * CHUNK, CHUNK)
        rows = pl.ds(r0, CHUNK)
        ball_all = _dot_sel_exact(sel_ref[...], loga_ref[rows, :])
        for h in range(GLA_HEADS):
            ks = slice(h * GLA_HEAD_K, (h + 1) * GLA_HEAD_K)
            vs = slice(h * GLA_HEAD_V, (h + 1) * GLA_HEAD_V)
            ball = ball_all[:, ks]
            b = ball[0:CHUNK]
            q = proj_ref[rows, pl.ds(GLA_COL_Q + h * GLA_HEAD_K, GLA_HEAD_K)] * scale
            k = proj_ref[rows, pl.ds(GLA_COL_K + h * GLA_HEAD_K, GLA_HEAD_K)]
            v = proj_ref[rows, pl.ds(GLA_COL_V + h * GLA_HEAD_V, GLA_HEAD_V)].astype(BF16)
            scores = jnp.where(diag, _dot_nt(q.astype(BF16), k.astype(BF16)), 0.0)
            for li in range(len(GLA_LEVELS)):
                d = b - ball[(li + 1) * CHUNK:(li + 2) * CHUNK]
                qe = q * jnp.exp(jnp.minimum(d, 0.0))
                ke = k * jnp.exp(jnp.minimum(-d, 0.0))
                scores = jnp.where(masks[li], _dot_nt(qe.astype(BF16), ke.astype(BF16)), scores)
            state = state_ref[h]
            o = _dot(scores.astype(BF16), v) + _dot((q * jnp.exp(b)).astype(BF16), state.astype(BF16))
            b_last = b[CHUNK - 1:CHUNK, :]
            kd = k * jnp.exp(b_last - b)
            decay = jnp.exp(jnp.broadcast_to(b_last, (GLA_HEAD_K, GLA_HEAD_K)).T)
            state_ref[h] = (jnp.concatenate([decay] * (GLA_HEAD_V // GLA_HEAD_K), axis=1) * state
                            + _dot_tn(kd.astype(BF16), v))
            o = o * lax.rsqrt(jnp.mean(o * o, axis=-1, keepdims=True) + RMS_EPS)
            r = proj_ref[rows, pl.ds(GLA_COL_R + h * GLA_HEAD_V, GLA_HEAD_V)]
            o_ref[rows, vs] = o * gn_ref[...] * _silu(r)
        return carry

    lax.fori_loop(0, tc // CHUNK, chunk, 0)


def _gla(x, w, wg_pad, bg, gn, bsz, seq, tc):
    nt = seq // tc
    t = bsz * seq
    kern = functools.partial(_gla_kernel, tc=tc)
    full = lambda shape: pl.BlockSpec(shape, lambda b, i: tuple(0 for _ in shape))
    return pl.pallas_call(
        kern,
        grid=(bsz, nt),
        in_specs=[
            pl.BlockSpec((tc, D_MODEL), lambda b, i: (b * nt + i, 0)),
            full((D_MODEL, GLA_PROJ_W)),
            full((LANES, GLA_DK)), full((1, GLA_DK)), full((1, GLA_HEAD_V)),
            full(((len(GLA_LEVELS) + 1) * CHUNK, CHUNK)),
        ],
        out_specs=pl.BlockSpec((tc, GLA_DV), lambda b, i: (b * nt + i, 0)),
        out_shape=jax.ShapeDtypeStruct((t, GLA_DV), F32),
        scratch_shapes=[pltpu.VMEM((GLA_HEADS, GLA_HEAD_K, GLA_HEAD_V), F32),
                        pltpu.VMEM((tc, GLA_DK), F32),
                        pltpu.VMEM((tc, GLA_PROJ_W), F32)],
        compiler_params=_cparams(("arbitrary", "arbitrary")),
        name="gla",
    )(x, w, wg_pad, bg, gn, _gla_select_matrix().astype(BF16))


CONV_PAD = SUBLANES


def _ssd_kernel(x_ref, w_ref, cw_ref, cb_ref, dtb_ref, aneg_ref, anegs_ref, dsk_ref, ng_ref, exp_ref,
                o_ref, xs_ref, z_ref, dt_ref, h_ref, yd_ref, *, tc):
    @pl.when(pl.program_id(1) == 0)
    def _():
        xs_ref[0:CONV_PAD, :] = jnp.zeros((CONV_PAD, SSD_CONV_DIM), F32)
        h_ref[...] = jnp.zeros_like(h_ref)

    xb = x_ref[...].astype(BF16)
    _project(xb, w_ref, xs_ref, CONV_PAD, SSD_COL_XBC, SSD_CONV_DIM)
    _project(xb, w_ref, z_ref, 0, SSD_COL_Z, SSD_DINNER)
    _project(xb, w_ref, dt_ref, 0, SSD_COL_DT, LANES)

    row = lax.broadcasted_iota(jnp.int32, (CHUNK, CHUNK), 0)
    col = lax.broadcasted_iota(jnp.int32, (CHUNK, CHUNK), 1)
    causal = row >= col
    tril = jnp.where(causal, 1.0, 0.0).astype(BF16)
    n0 = SSD_DINNER
    n1 = SSD_DINNER + SSD_GROUPS * SSD_STATE

    def chunk(c):
        r0 = c * CHUNK
        rows = pl.ds(r0, CHUNK)
        conv = cb_ref[...]
        for kk in range(SSD_CONV):
            off = CONV_PAD - (SSD_CONV - 1) + kk
            conv = conv + cw_ref[kk:kk + 1, :] * xs_ref[pl.ds(r0 + off, CHUNK), :]
        xbc = _silu(conv)
        x = xbc[:, 0:n0]
        dt = _softplus(dt_ref[rows, :] + dtb_ref[...])
        dtc = _dot_exact_sel(dt, exp_ref[...])
        cs = _dot_sel_exact(tril, dtc * aneg_ref[...])
        cs_t = _dot_sel_exact(tril, dt * anegs_ref[...]).T
        xdt = x * dtc
        cs_last = cs[CHUNK - 1:CHUNK, :]
        xw = (xdt * jnp.exp(cs_last - cs)).astype(BF16)
        xdt_b = xdt.astype(BF16)
        ecs = jnp.exp(cs)
        ecl = jnp.exp(cs_last)
        y_parts = []
        for g in range(SSD_GROUPS):
            lo = g * SSD_GROUP_W
            bg = xbc[:, n0 + g * SSD_STATE:n0 + (g + 1) * SSD_STATE].astype(BF16)
            cg = xbc[:, n1 + g * SSD_STATE:n1 + (g + 1) * SSD_STATE].astype(BF16)
            cb = _dot_nt(cg, bg)
            hg = h_ref[g]
            y_off = _dot(cg, hg.astype(BF16)) * ecs[:, lo:lo + SSD_GROUP_W]
            for hh in range(SSD_HPG):
                hd = g * SSD_HPG + hh
                c0 = lo + hh * SSD_HEADDIM
                colv = cs[:, c0:c0 + SSD_HEADDIM]
                rowv = jnp.broadcast_to(cs_t[hd:hd + 1, :], (CHUNK, CHUNK))
                lm = jnp.exp(jnp.where(causal, colv - rowv, -jnp.inf))
                yd_ref[:, c0:c0 + SSD_HEADDIM] = _dot((cb * lm).astype(BF16), xdt_b[:, c0:c0 + SSD_HEADDIM])
            states = _dot_tn(bg, xw[:, lo:lo + SSD_GROUP_W])
            h_ref[g] = hg * ecl[:, lo:lo + SSD_GROUP_W] + states
            y_parts.append(y_off)
        y = yd_ref[...] + jnp.concatenate(y_parts, axis=1) + dsk_ref[...] * x
        y = y * _silu(z_ref[rows, :])
        outs = []
        for g in range(SSD_GROUPS):
            yg = y[:, g * SSD_GROUP_W:(g + 1) * SSD_GROUP_W]
            outs.append(yg * lax.rsqrt(jnp.mean(yg * yg, axis=-1, keepdims=True) + RMS_EPS))
        o_ref[rows, :] = jnp.concatenate(outs, axis=1) * ng_ref[...]

    for c in range(tc // CHUNK):
        chunk(c)
    xs_ref[0:CONV_PAD, :] = xs_ref[tc:tc + CONV_PAD, :]


def _ssd(x, w, cw, cb, dtb, aneg, anegs, dsk, ng, expand, bsz, seq, tc):
    nt = seq // tc
    t = bsz * seq
    kern = functools.partial(_ssd_kernel, tc=tc)
    full = lambda shape: pl.BlockSpec(shape, lambda b, i: tuple(0 for _ in shape))
    return pl.pallas_call(
        kern,
        grid=(bsz, nt),
        in_specs=[
            pl.BlockSpec((tc, D_MODEL), lambda b, i: (b * nt + i, 0)),
            full((D_MODEL, SSD_PROJ_W)),
            full((SSD_CONV, SSD_CONV_DIM)), full((1, SSD_CONV_DIM)), full((1, LANES)),
            full((1, SSD_DINNER)), full((1, LANES)), full((1, SSD_DINNER)), full((1, SSD_DINNER)),
            full((LANES, SSD_DINNER)),
        ],
        out_specs=pl.BlockSpec((tc, SSD_DINNER), lambda b, i: (b * nt + i, 0)),
        out_shape=jax.ShapeDtypeStruct((t, SSD_DINNER), F32),
        scratch_shapes=[pltpu.VMEM((tc + CONV_PAD, SSD_CONV_DIM), F32),
                        pltpu.VMEM((tc, SSD_DINNER), F32),
                        pltpu.VMEM((tc, LANES), F32),
                        pltpu.VMEM((SSD_GROUPS, SSD_STATE, SSD_GROUP_W), F32),
                        pltpu.VMEM((CHUNK, SSD_DINNER), F32)],
        compiler_params=_cparams(("arbitrary", "arbitrary")),
        name="ssd",
    )(x, w, cw, cb, dtb, aneg, anegs, dsk, ng, expand)


def _outproj_kernel(o_ref, y_ref, x_ref, w_ref, g_ref, b_ref, h_ref):
    mix = (_dot(o_ref[...].astype(BF16), w_ref[0:GLA_DV, :])
           + _dot(y_ref[...].astype(BF16), w_ref[GLA_DV:GLA_DV + SSD_DINNER, :]))
    h_ref[...] = _layer_norm(DEEPNORM_ALPHA * x_ref[...] + mix, g_ref[...], b_ref[...])


def _outproj(o, y, x, w, g, b, tm):
    t = x.shape[0]
    tile = lambda w_: pl.BlockSpec((tm, w_), lambda i: (i, 0))
    full = lambda shape: pl.BlockSpec(shape, lambda i: tuple(0 for _ in shape))
    return pl.pallas_call(
        _outproj_kernel,
        grid=(t // tm,),
        in_specs=[tile(GLA_DV), tile(SSD_DINNER), tile(D_MODEL), full(w.shape),
                  full((1, D_MODEL)), full((1, D_MODEL))],
        out_specs=tile(D_MODEL),
        out_shape=jax.ShapeDtypeStruct((t, D_MODEL), F32),
        compiler_params=_cparams(("arbitrary",)),
        name="outproj_ln",
    )(o, y, x, w, g, b)


def _xattn_kernel(h_ref, k_ref, v_ref, wq_ref, wo_ref, g_ref, b_ref, o_ref):
    h = h_ref[...]
    q = _dot(h.astype(BF16), wq_ref[...]).astype(BF16)
    scale = MEM_HEAD_DIM ** -0.5
    heads = []
    for hd in range(MEM_HEADS):
        sl = slice(hd * MEM_HEAD_DIM, (hd + 1) * MEM_HEAD_DIM)
        s = _dot_nt(q[:, sl], k_ref[:, sl].astype(BF16)) * scale
        s = s - jnp.max(s, axis=-1, keepdims=True)
        e = jnp.exp(s)
        p = e / jnp.sum(e, axis=-1, keepdims=True)
        heads.append(_dot(p.astype(BF16), v_ref[:, sl].astype(BF16)))
    ca = _dot(jnp.concatenate(heads, axis=1).astype(BF16), wo_ref[...])
    o_ref[...] = _layer_norm(DEEPNORM_ALPHA * h + ca, g_ref[...], b_ref[...])


def _xattn(h, kv, wq, wo, g, b, bsz, seq, mem_len, tm):
    nt = seq // tm
    t = bsz * seq
    full = lambda shape: pl.BlockSpec(shape, lambda bb, i: tuple(0 for _ in shape))
    return pl.pallas_call(
        _xattn_kernel,
        grid=(bsz, nt),
        in_specs=[pl.BlockSpec((tm, D_MODEL), lambda bb, i: (bb * nt + i, 0)),
                  pl.BlockSpec((mem_len, D_MODEL), lambda bb, i: (bb, 0)),
                  pl.BlockSpec((mem_len, D_MODEL), lambda bb, i: (bb, 1)),
                  full((D_MODEL, D_MODEL)), full((D_MODEL, D_MODEL)),
                  full((1, D_MODEL)), full((1, D_MODEL))],
        out_specs=pl.BlockSpec((tm, D_MODEL), lambda bb, i: (bb * nt + i, 0)),
        out_shape=jax.ShapeDtypeStruct((t, D_MODEL), F32),
        compiler_params=_cparams(("arbitrary", "arbitrary")),
        name="xattn_ln",
    )(h, kv, kv, wq, wo, g, b)


N_SEL = PEER_HEADS * PEER_TOPK
ACT_N_TILE = 256
TOK_GROUP = 32
NEG_INF = float("-inf")

_CAND_GROUPS = ((0, 0, 8), (0, 8, 8), (1, 0, 8), (2, 0, 5), (3, 0, 4), (4, 0, 3), (5, 0, 2), (6, 0, 2), (7, 0, 2))


def _oddeven_merge_sort_pairs(n):
    pairs = []
    p = 1
    while p < n:
        k = p
        while k >= 1:
            for j in range(k % p, n - k, 2 * k):
                for i in range(min(k, n - j - k)):
                    if (i + j) // (2 * p) == (i + j + k) // (2 * p):
                        pairs.append((i + j, i + j + k))
            k //= 2
        p *= 2
    return tuple(pairs)


_SORT16 = _oddeven_merge_sort_pairs(PEER_N_KEYS // SUBLANES)


def _peer_select_kernel(h_ref, wq_ref, keys_ref, i1_ref, i2_ref, g_ref, hb_ref,
                        st_ref, it_ref, i1s_ref, i2s_ref, gs_ref, sc_ref, *, tt):
    hb = h_ref[...].astype(BF16)
    hb_ref[...] = hb
    q = _dot(hb, wq_ref[...]).astype(BF16)
    key_iota = lax.broadcasted_iota(jnp.int32, (PEER_N_KEYS, tt), 0).astype(F32)
    row8 = lax.broadcasted_iota(jnp.int32, (SUBLANES, tt), 0).astype(F32)
    row16 = lax.broadcasted_iota(jnp.int32, (PEER_TOPK, tt), 0).astype(F32)
    big = float(PEER_TOPK * PEER_TOPK)

    def exact_topk(x, hp):
        for kk in range(PEER_TOPK):
            m = jnp.max(x, axis=0, keepdims=True)
            idx = jnp.min(jnp.where(x == m, key_iota, float(PEER_N_KEYS)), axis=0, keepdims=True)
            x = jnp.where(key_iota == idx, NEG_INF, x)
            st_ref[hp, kk:kk + 1, :] = m
            it_ref[hp, kk:kk + 1, :] = idx

    def sorted_topk(hp, c0):
        n = PEER_N_KEYS // SUBLANES
        lanes = slice(c0, c0 + LANES)
        vs = [sc_ref[hp, j * SUBLANES:(j + 1) * SUBLANES, lanes] for j in range(n)]
        ids = [row8[:, 0:LANES] + float(j * SUBLANES) for j in range(n)]

        def cex(i, j):
            c = vs[i] > vs[j]
            hi_i, lo_i = jnp.where(c, ids[i], ids[j]), jnp.where(c, ids[j], ids[i])
            vs[i], vs[j] = jnp.maximum(vs[i], vs[j]), jnp.minimum(vs[i], vs[j])
            ids[i], ids[j] = hi_i, lo_i

        for (i, j) in _SORT16:
            cex(i, j)
        for shift in (SUBLANES // 2, SUBLANES // 4, SUBLANES // 8):
            rv = [pltpu.roll(v, shift, axis=0) for v in vs]
            ri = [pltpu.roll(v, shift, axis=0) for v in ids]
            for i in range(n):
                c = vs[i] > rv[n - 1 - i]
                ids[i] = jnp.where(c, ids[i], ri[n - 1 - i])
                vs[i] = jnp.maximum(vs[i], rv[n - 1 - i])
            d = n // 2
            while d >= 1:
                for i in range(n):
                    if i & d == 0:
                        cex(i, i + d)
                d //= 2
        bad = jnp.zeros((SUBLANES, LANES), F32)
        for kk in range(PEER_TOPK - 1):
            bad = jnp.where(vs[kk] > vs[kk + 1], bad, 1.0)
        top_v = [v[0:1, :] for v in vs[:PEER_TOPK]]
        top_i = [v[0:1, :] for v in ids[:PEER_TOPK]]
        count = jnp.sum(jnp.where(sc_ref[hp, :, lanes] >= top_v[PEER_TOPK - 1], 1.0, 0.0), axis=0, keepdims=True)
        bad = jnp.maximum(bad, jnp.where(count == float(PEER_TOPK), 0.0, 1.0))
        for kk in range(PEER_TOPK):
            st_ref[hp, kk:kk + 1, lanes] = top_v[kk]
            it_ref[hp, kk:kk + 1, lanes] = top_i[kk]
        return jnp.max(bad)

    n_hp = PEER_HEADS * 2
    has_tie = []
    for hp in range(n_hp):
        sc_ref[hp] = _dot_nt(keys_ref[hp % 2], q[:, hp * PEER_HALF:(hp + 1) * PEER_HALF])
        has_tie.append(functools.reduce(jnp.maximum, [sorted_topk(hp, c0) for c0 in range(0, tt, LANES)]))
    for hp in range(n_hp):
        @pl.when(has_tie[hp] > 0.0)
        def _():
            exact_topk(sc_ref[hp], hp)

    for head in range(PEER_HEADS):
        s1 = st_ref[2 * head]
        s2 = st_ref[2 * head + 1]
        vals, flats = [], []
        for (a, b0, cnt) in _CAND_GROUPS:
            v = s1[a:a + 1, :] + s2[b0:b0 + SUBLANES, :]
            vals.append(jnp.where(row8 < float(cnt), v, NEG_INF))
            flats.append(float(a * PEER_TOPK + b0) + row8)
        vals.append(s1[SUBLANES:PEER_TOPK, :] + s2[0:1, :])
        flats.append((float(SUBLANES) + row8) * float(PEER_TOPK))
        best_rows = []
        for kk in range(PEER_TOPK):
            m = functools.reduce(jnp.maximum, vals)
            m = jnp.max(m, axis=0, keepdims=True)
            cand = functools.reduce(jnp.minimum,
                                    [jnp.where(v == m, f, big) for v, f in zip(vals, flats)])
            cidx = jnp.min(cand, axis=0, keepdims=True)
            vals = [jnp.where(f == cidx, NEG_INF, v) for v, f in zip(vals, flats)]
            a_sel = jnp.floor(cidx * (1.0 / PEER_TOPK))
            b_sel = cidx - a_sel * float(PEER_TOPK)
            i1 = jnp.sum(jnp.where(row16 == a_sel, it_ref[2 * head], 0.0), axis=0, keepdims=True)
            i2 = jnp.sum(jnp.where(row16 == b_sel, it_ref[2 * head + 1], 0.0), axis=0, keepdims=True)
            r = head * PEER_TOPK + kk
            i1s_ref[r:r + 1, :] = i1
            i2s_ref[r:r + 1, :] = i2
            best_rows.append(m)
        e = [jnp.exp(b - best_rows[0]) for b in best_rows]
        denom = functools.reduce(lambda u, w: u + w, e)
        for kk in range(PEER_TOPK):
            r = head * PEER_TOPK + kk
            gs_ref[r:r + 1, :] = e[kk] / denom

    i1_ref[...] = i1s_ref[...].T.astype(jnp.int32)
    i2_ref[...] = i2s_ref[...].T.astype(jnp.int32)
    g_ref[...] = gs_ref[...].T


def _peer_select(h, wq, keys, tt):
    t = h.shape[0]
    kern = functools.partial(_peer_select_kernel, tt=tt)
    tile = lambda w_: pl.BlockSpec((tt, w_), lambda i: (i, 0))
    full = lambda shape: pl.BlockSpec(shape, lambda i: tuple(0 for _ in shape))
    return pl.pallas_call(
        kern,
        grid=(t // tt,),
        in_specs=[tile(D_MODEL), full((D_MODEL, D_MODEL)), full((2, PEER_N_KEYS, PEER_HALF))],
        out_specs=[tile(N_SEL), tile(N_SEL), tile(N_SEL), tile(D_MODEL)],
        out_shape=[jax.ShapeDtypeStruct((t, N_SEL), jnp.int32),
                   jax.ShapeDtypeStruct((t, N_SEL), jnp.int32),
                   jax.ShapeDtypeStruct((t, N_SEL), F32),
                   jax.ShapeDtypeStruct((t, D_MODEL), BF16)],
        scratch_shapes=[pltpu.VMEM((2 * PEER_HEADS, PEER_TOPK, tt), F32),
                        pltpu.VMEM((2 * PEER_HEADS, PEER_TOPK, tt), F32),
                        pltpu.VMEM((N_SEL, tt), F32),
                        pltpu.VMEM((N_SEL, tt), F32),
                        pltpu.VMEM((N_SEL, tt), F32),
                        pltpu.VMEM((2 * PEER_HEADS, PEER_N_KEYS, tt), F32)],
        compiler_params=_cparams(("arbitrary",)),
        name="peer_select",
    )(h, wq, keys)


def _peer_act_kernel(hb_ref, u_ref, i1_ref, i2_ref, val_ref, *, eb):
    hb = hb_ref[...]
    i1 = i1_ref[...]
    i2 = i2_ref[...]
    slabs = eb // PEER_N_KEYS
    slab0 = pl.program_id(0) * slabs
    acc = jnp.zeros(i1.shape, F32)
    for c in range(eb // ACT_N_TILE):
        act = _dot_nt(hb, u_ref[c * ACT_N_TILE:(c + 1) * ACT_N_TILE, :])
        for s in range(ACT_N_TILE // PEER_N_KEYS):
            slab = c * (ACT_N_TILE // PEER_N_KEYS) + s
            picked = jnp.take_along_axis(act[:, s * PEER_N_KEYS:(s + 1) * PEER_N_KEYS], i2, axis=1)
            acc = jnp.where(i1 == slab0 + slab, picked, acc)
    val_ref[0] = acc


def _peer_act(hb, u, i1, i2, tt, eb):
    t = hb.shape[0]
    ne = u.shape[0] // eb
    kern = functools.partial(_peer_act_kernel, eb=eb)
    return pl.pallas_call(
        kern,
        grid=(ne, t // tt),
        in_specs=[pl.BlockSpec((tt, D_MODEL), lambda e, i: (i, 0)),
                  pl.BlockSpec((eb, D_MODEL), lambda e, i: (e, 0)),
                  pl.BlockSpec((tt, N_SEL), lambda e, i: (i, 0)),
                  pl.BlockSpec((tt, N_SEL), lambda e, i: (i, 0))],
        out_specs=pl.BlockSpec((1, tt, N_SEL), lambda e, i: (e, i, 0)),
        out_shape=jax.ShapeDtypeStruct((ne, t, N_SEL), F32),
        compiler_params=_cparams(("arbitrary", "arbitrary")),
        name="peer_act",
    )(hb, u, i1, i2)


def _slab_pitch(tt):
    octets = tt // SUBLANES
    return SUBLANES * (octets + 1 if octets % 2 == 0 else octets)


def _peer_out_kernel(val_ref, g_ref, i1_ref, i2_ref, v_ref, h_ref, lg_ref, lb_ref, o_ref,
                     w_ref, wslab_ref, *, tt, eb, ne):
    j = pl.program_id(1)
    pitch = _slab_pitch(tt)

    @pl.when(j == 0)
    def _():
        val = val_ref[0]
        for e in range(1, val_ref.shape[0]):
            val = val + val_ref[e]
        gelu = 0.5 * val * (1.0 + lax.erf(val * (2.0 ** -0.5)))
        w_ref[...] = g_ref[...] * gelu
        o_ref[...] = DEEPNORM_ALPHA * h_ref[...]
        sub = lax.broadcasted_iota(jnp.int32, (PEER_N_KEYS, N_SEL), 0).astype(F32).astype(BF16)
        one = jnp.ones((PEER_N_KEYS, N_SEL), BF16)
        zero = jnp.zeros((PEER_N_KEYS, N_SEL), BF16)

        def tok_group(gi, carry):
            t0 = pl.multiple_of(gi * TOK_GROUP, TOK_GROUP)
            i1g = i1_ref[pl.ds(t0, TOK_GROUP), :].astype(F32).astype(BF16)
            i2g = i2_ref[pl.ds(t0, TOK_GROUP), :].astype(F32).astype(BF16)
            wg = w_ref[pl.ds(t0, TOK_GROUP), :].astype(BF16)
            for u in range(TOK_GROUP):
                p1 = jnp.where(sub == i1g[u:u + 1, :], one, zero)
                p2w = jnp.where(sub == i2g[u:u + 1, :], wg[u:u + 1, :], zero)
                g = _dot_nt(p1, p2w)
                for o in range(PEER_N_KEYS // SUBLANES):
                    wslab_ref[pl.ds(o * SUBLANES * pitch + t0 + u, SUBLANES, stride=pitch), :] = (
                        g[o * SUBLANES:(o + 1) * SUBLANES, :])
            return carry

        lax.fori_loop(0, tt // TOK_GROUP, tok_group, 0)

    slabs = eb // PEER_N_KEYS
    cols = []
    for s in range(slabs):
        r0 = pl.multiple_of((j * slabs + s) * pitch, SUBLANES)
        cols.append(wslab_ref[pl.ds(r0, tt), :].astype(BF16))
    o_ref[...] += _dot(jnp.concatenate(cols, axis=1), v_ref[...])

    @pl.when(j == ne - 1)
    def _():
        o_ref[...] = _layer_norm(o_ref[...], lg_ref[...], lb_ref[...])


def _peer_out(val, gates, i1, i2, v, h, lg, lb, tt, eb):
    t = h.shape[0]
    ne = v.shape[0] // eb
    kern = functools.partial(_peer_out_kernel, tt=tt, eb=eb, ne=ne)
    once = pl.Buffered(1)
    tile = lambda w_: pl.BlockSpec((tt, w_), lambda i, j: (i, 0), pipeline_mode=once)
    full = lambda shape: pl.BlockSpec(shape, lambda i, j: tuple(0 for _ in shape))
    return pl.pallas_call(
        kern,
        grid=(t // tt, ne),
        in_specs=[pl.BlockSpec((val.shape[0], tt, N_SEL), lambda i, j: (0, i, 0), pipeline_mode=once),
                  tile(N_SEL), tile(N_SEL), tile(N_SEL),
                  pl.BlockSpec((eb, D_MODEL), lambda i, j: (j, 0)),
                  tile(D_MODEL), full((1, D_MODEL)), full((1, D_MODEL))],
        out_specs=pl.BlockSpec((tt, D_MODEL), lambda i, j: (i, 0)),
        out_shape=jax.ShapeDtypeStruct((t, D_MODEL), F32),
        scratch_shapes=[pltpu.VMEM((tt, N_SEL), F32),
                        pltpu.VMEM((PEER_N_KEYS * _slab_pitch(tt), PEER_N_KEYS), F32)],
        compiler_params=_cparams(("arbitrary", "arbitrary")),
        name="peer_out",
    )(val, gates, i1, i2, v, h, lg, lb)


def _pad_lanes(a, width):
    return jnp.pad(a, ((0, 0), (0, width - a.shape[1])))


def _regroup_w_in(w_in):
    sizes = (GLA_DK, GLA_DK, GLA_DV, GLA_DV, GLA_GATE_RANK, SSD_DINNER, SSD_CONV_DIM, SSD_HEADS)
    offs = [0]
    for s in sizes:
        offs.append(offs[-1] + s)
    q, k, v, r, glow, z, xbc, dt = (w_in[:, offs[i]:offs[i + 1]] for i in range(len(sizes)))
    w_gla = jnp.concatenate([v, r, q, k, _pad_lanes(glow, LANES)], axis=1).astype(BF16)
    w_ssd = jnp.concatenate([xbc, z, _pad_lanes(dt, LANES)], axis=1).astype(BF16)
    return w_gla, w_ssd


def _layer(h, mem, w_in, gla_wg2, gla_bg, gla_norm_g, conv_w, conv_b, dt_bias, a_log, d_skip, ssd_norm_g,
           w_out, ln1_g, ln1_b, ca_wq, ca_wk, ca_wv, ca_wo, ln2_g, ln2_b, peer_wq, peer_keys, peer_u,
           peer_v, ln3_g, ln3_b, bsz, seq):
    t = bsz * seq
    tm = min(512, seq)
    tc = min(512, seq)
    row = lambda a: a.reshape(1, -1)

    w_gla, w_ssd = _regroup_w_in(w_in)
    wg_pad = jnp.pad(gla_wg2, ((0, LANES - GLA_GATE_RANK), (0, 0))).astype(BF16)
    o = _gla(h, w_gla, wg_pad, row(gla_bg), row(gla_norm_g), bsz, seq, tc)
    rep = lambda a: jnp.repeat(a.astype(F32), SSD_HEADDIM).reshape(1, SSD_DINNER)
    expand = (jnp.arange(LANES)[:, None] == (jnp.arange(SSD_DINNER)[None, :] // SSD_HEADDIM)).astype(BF16)
    a_neg = -jnp.exp(a_log.astype(F32))
    y = _ssd(h, w_ssd, conv_w, row(conv_b), _pad_lanes(row(dt_bias), LANES), rep(a_neg),
             _pad_lanes(row(a_neg), LANES), rep(d_skip), row(ssd_norm_g), expand, bsz, seq, tc)
    h1 = _outproj(o, y, h, w_out.astype(BF16), row(ln1_g), row(ln1_b), tm)

    mem_len = mem.shape[1]
    mem2 = mem.reshape(bsz * mem_len, D_MODEL)
    kv = _matmul(mem2, jnp.concatenate([ca_wk, ca_wv], axis=1).astype(BF16), min(512, bsz * mem_len), 512)
    h2 = _xattn(h1, kv, ca_wq.astype(BF16), ca_wo.astype(BF16),
                row(ln2_g), row(ln2_b), bsz, seq, mem_len, tm)

    tt = min(LANES, t)
    i1, i2, gates, hb = _peer_select(h2, peer_wq.astype(BF16), peer_keys.astype(BF16), tt)
    eb = 2048
    val = _peer_act(hb, peer_u.astype(BF16), i1, i2, min(512, t), eb)
    return _peer_out(val, gates, i1, i2, peer_v.astype(BF16), h2, row(ln3_g), row(ln3_b), min(512, t), 1024)


def kernel(x, mem, w_in, gla_wg2, gla_bg, gla_norm_g, ssd_conv_w, ssd_conv_b, ssd_dt_bias, ssd_a_log, ssd_d,
           ssd_norm_g, w_out, ln1_g, ln1_b, ca_wq, ca_wk, ca_wv, ca_wo, ln2_g, ln2_b, peer_wq, peer_keys,
           peer_u, peer_v, ln3_g, ln3_b):
    bsz, seq, _ = x.shape
    h = x.reshape(bsz * seq, D_MODEL)
    for l in range(w_in.shape[0]):
        h = _layer(h, mem, w_in[l], gla_wg2[l], gla_bg[l], gla_norm_g[l], ssd_conv_w[l], ssd_conv_b[l],
                   ssd_dt_bias[l], ssd_a_log[l], ssd_d[l], ssd_norm_g[l], w_out[l], ln1_g[l], ln1_b[l],
                   ca_wq[l], ca_wk[l], ca_wv[l], ca_wo[l], ln2_g[l], ln2_b[l], peer_wq[l], peer_keys[l],
                   peer_u[l], peer_v[l], ln3_g[l], ln3_b[l], bsz, seq)
    return h.reshape(bsz, seq, D_MODEL)
```
